```python
import math
import jax, jax.numpy as jnp
from jax import lax
import numpy as np

D_MODEL = 1024
BATCH = 4
SEQ = 8192
DEPTH = 1
DEC_BATCH = 32
DEC_SEQ = 2048
PAST_LEN = 128

N_FGROUPS = 4
FGROUP_DIM = 64
F_WIDTH = N_FGROUPS * FGROUP_DIM
N_HEADS = 6
QK_NOPE_DIM = 128
QK_ROPE_DIM = 64
V_HEAD_DIM = 128
QK_HEAD_DIM = QK_NOPE_DIM + QK_ROPE_DIM
Q_LORA_RANK = 384
KV_LORA_RANK = 256
ATTN_WIDTH = N_HEADS * V_HEAD_DIM
MIX_WIDTH = F_WIDTH + ATTN_WIDTH
IN_WIDTH = F_WIDTH + Q_LORA_RANK + KV_LORA_RANK + QK_ROPE_DIM
D_FF = ((8 * D_MODEL // 3 + 255) // 256) * 256
ROPE_THETA = 10000.0
EPS = 1e-6
Q_BLOCK = 128
SM_SCALE = 1.0 / math.sqrt(QK_HEAD_DIM)

kernel_name = "fnet_mla_parallel_encoder"


def rms_norm(x, g):
    xf = x.astype(jnp.float32)
    y = xf * lax.rsqrt(jnp.mean(xf * xf, axis=-1, keepdims=True) + EPS)
    return (y * g.astype(jnp.float32)).astype(x.dtype)


def rope_tables(seq_len):
    inv_freq = 1.0 / (ROPE_THETA ** (jnp.arange(0, QK_ROPE_DIM, 2, dtype=jnp.float32) / QK_ROPE_DIM))
    ang = jnp.arange(seq_len, dtype=jnp.float32)[:, None] * inv_freq[None, :]
    return jnp.cos(ang), jnp.sin(ang)


def apply_rope(x, cos, sin):
    xf = x.astype(jnp.float32)
    x1, x2 = jnp.split(xf, 2, axis=-1)
    return jnp.concatenate([x1 * cos - x2 * sin, x2 * cos + x1 * sin], axis=-1).astype(x.dtype)


def fourier_mix(u):
    b, s, _ = u.shape
    ug = u.reshape(b, s, N_FGROUPS, FGROUP_DIM).astype(jnp.float32)
    f = jnp.fft.fft2(ug, axes=(1, 3), norm="ortho").real
    return f.reshape(b, s, F_WIDTH).astype(u.dtype)


def latent_attention(q_nope, q_rope, k_nope, k_rope, v):
    b, s, h, _ = q_nope.shape
    nb = s // Q_BLOCK
    qn = q_nope.reshape(b, nb, Q_BLOCK, h, QK_NOPE_DIM).swapaxes(0, 1)
    qr = q_rope.reshape(b, nb, Q_BLOCK, h, QK_ROPE_DIM).swapaxes(0, 1)

    def block(args):
        qn_b, qr_b = args
        sc = (jnp.einsum('bqhd,bkhd->bhqk', qn_b, k_nope, preferred_element_type=jnp.float32)
              + jnp.einsum('bqhr,bkr->bhqk', qr_b, k_rope, preferred_element_type=jnp.float32)) * SM_SCALE
        p = jax.nn.softmax(sc, axis=-1)
        return jnp.einsum('bhqk,bkhd->bqhd', p.astype(v.dtype), v)

    o = lax.map(block, (qn, qr))
    return o.swapaxes(0, 1).reshape(b, s, h * V_HEAD_DIM)


def token_mixer(xn, w_in, q_norm_g, w_q_up, kv_norm_g, w_kv_up, w_out):
    b, s, _ = xn.shape
    hcat = xn @ w_in
    o1 = F_WIDTH
    o2 = o1 + Q_LORA_RANK
    o3 = o2 + KV_LORA_RANK
    u_f = hcat[..., :o1]
    c_q = hcat[..., o1:o2]
    c_kv = hcat[..., o2:o3]
    k_rope_raw = hcat[..., o3:]
    y_f = fourier_mix(u_f)
    cos, sin = rope_tables(s)
    q = (rms_norm(c_q, q_norm_g) @ w_q_up).reshape(b, s, N_HEADS, QK_HEAD_DIM)
    q_nope = q[..., :QK_NOPE_DIM]
    q_rope = apply_rope(q[..., QK_NOPE_DIM:], cos[:, None, :], sin[:, None, :])
    kv = (rms_norm(c_kv, kv_norm_g) @ w_kv_up).reshape(b, s, N_HEADS, QK_NOPE_DIM + V_HEAD_DIM)
    k_nope = kv[..., :QK_NOPE_DIM]
    v = kv[..., QK_NOPE_DIM:]
    k_rope = apply_rope(k_rope_raw, cos, sin)
    y_a = latent_attention(q_nope, q_rope, k_nope, k_rope, v)
    return jnp.concatenate([y_f, y_a], axis=-1) @ w_out


def swiglu(xn, w_gate, w_up, w_down):
    return (jax.nn.silu(xn @ w_gate) * (xn @ w_up)) @ w_down


def setup_inputs(seed: int = 0) -> dict:
    key = jax.random.key(seed)
    ks = jax.random.split(key, 16)
    f32 = jnp.float32

    def w(k, shape, fan_in):
        return jax.random.normal(k, shape, f32) * (fan_in ** -0.5)

    def gain(k, shape):
        return 1.0 + 0.02 * jax.random.normal(k, shape, f32)

    return {
        "x_prompt": jax.random.normal(ks[0], (BATCH, SEQ, D_MODEL), f32),
        "x_sample": jax.random.normal(ks[1], (DEC_BATCH, DEC_SEQ, D_MODEL), f32),
        "norm_mix_g": gain(ks[2], (DEPTH, D_MODEL)),
        "w_in": w(ks[3], (DEPTH, D_MODEL, IN_WIDTH), D_MODEL),
        "q_norm_g": gain(ks[4], (DEPTH, Q_LORA_RANK)),
        "w_q_up": w(ks[5], (DEPTH, Q_LORA_RANK, N_HEADS * QK_HEAD_DIM), Q_LORA_RANK),
        "kv_norm_g": gain(ks[6], (DEPTH, KV_LORA_RANK)),
        "w_kv_up": w(ks[7], (DEPTH, KV_LORA_RANK, N_HEADS * (QK_NOPE_DIM + V_HEAD_DIM)), KV_LORA_RANK),
        "w_out": w(ks[8], (DEPTH, MIX_WIDTH, D_MODEL), MIX_WIDTH),
        "norm_ffn_g": gain(ks[9], (DEPTH, D_MODEL)),
        "w_gate": w(ks[10], (DEPTH, D_MODEL, D_FF), D_MODEL),
        "w_up": w(ks[11], (DEPTH, D_MODEL, D_FF), D_MODEL),
        "w_down": w(ks[12], (DEPTH, D_FF, D_MODEL), D_FF),
        "final_norm_g": gain(ks[13], (D_MODEL,)),
    }


def reference(x_prompt, x_sample, norm_mix_g, w_in, q_norm_g, w_q_up, kv_norm_g, w_kv_up, w_out,
              norm_ffn_g, w_gate, w_up, w_down, final_norm_g):
    def trunk(x):
        for l in range(DEPTH):
            x = x + token_mixer(rms_norm(x, norm_mix_g[l]), w_in[l], q_norm_g[l], w_q_up[l],
                                kv_norm_g[l], w_kv_up[l], w_out[l])
            x = x + swiglu(rms_norm(x, norm_ffn_g[l]), w_gate[l], w_up[l], w_down[l])
        return rms_norm(x, final_norm_g)

    y_prompt = trunk(x_prompt)
    y_sample = trunk(x_sample)
    return (y_prompt, y_sample)
```

```python
import functools
import math

import jax
import jax.numpy as jnp
from jax import lax
from jax.experimental import pallas as pl
from jax.experimental.pallas import tpu as pltpu

LANES = 128
MXU_DIM = 256
VMEM_BYTES = 64 * 1024 * 1024

N_FGROUPS = 4
FGROUP_DIM = 64
F_WIDTH = N_FGROUPS * FGROUP_DIM
N_HEADS = 6
QK_NOPE_DIM = 128
QK_ROPE_DIM = 64
V_HEAD_DIM = 128
QK_HEAD_DIM = QK_NOPE_DIM + QK_ROPE_DIM
ATTN_WIDTH = N_HEADS * V_HEAD_DIM
ROPE_THETA = 10000.0
EPS = 1e-6
SM_SCALE = 1.0 / math.sqrt(QK_HEAD_DIM)
LOG2E = 1.4426950408889634

HEAD_PAD = 2 * LANES
FFT_S1 = MXU_DIM

BF16 = jnp.bfloat16
F32 = jnp.float32


def _vmem_limit(nbytes):
    return int(min(nbytes, VMEM_BYTES - 4 * 1024 * 1024))


def _rms(x, g):
    return x * lax.rsqrt(jnp.mean(x * x, axis=-1, keepdims=True) + EPS) * g


def _dot(a, b):
    return jnp.dot(a, b, preferred_element_type=F32)


def _proj_kernel(x_ref, g_ref, w1_ref, d64_ref, qg_ref, wq_ref, kvg_ref, wkv_ref, cs_ref,
                 z_ref, q_ref, k_ref, v_ref):
    o1 = F_WIDTH
    o2 = o1 + qg_ref.shape[1]
    o3 = o2 + kvg_ref.shape[1]
    xn = _rms(x_ref[0], g_ref[...]).astype(BF16)
    h = _dot(xn, w1_ref[...])

    z = _dot(h[:, :o1].astype(BF16), d64_ref[...])
    z_ref[0, 0] = z[:, :o1].astype(BF16)
    z_ref[0, 1] = z[:, o1:].astype(BF16)

    cs = cs_ref[...]
    cqn = (_rms(h[:, o1:o2], qg_ref[...]) * (SM_SCALE * LOG2E)).astype(BF16)
    qf = _dot(cqn, wq_ref[...])
    for hh in range(N_HEADS):
        base = hh * HEAD_PAD
        q_ref[0, hh, :, :LANES] = qf[:, base:base + LANES].astype(BF16)
        q_ref[0, hh, :, LANES:] = (qf[:, base + LANES:base + HEAD_PAD] * cs).astype(BF16)

    ckvn = _rms(h[:, o2:o3], kvg_ref[...]).astype(BF16)
    kv = _dot(ckvn, wkv_ref[...])
    pk = h[:, o3:] * cs
    kr = (pk + pltpu.roll(pk, QK_ROPE_DIM, axis=1)).astype(BF16)
    ones = jnp.ones((kv.shape[0], LANES), BF16)
    for hh in range(N_HEADS):
        k_ref[0, hh, :, :LANES] = kv[:, hh * LANES:(hh + 1) * LANES].astype(BF16)
        k_ref[0, hh, :, LANES:] = kr
        v_ref[0, hh, :, :LANES] = kv[:, ATTN_WIDTH + hh * LANES:ATTN_WIDTH + (hh + 1) * LANES].astype(BF16)
        v_ref[0, hh, :, LANES:] = ones


def _proj(x, g, w1, d64, qg, wq, kvg, wkv, cs, *, tm):
    b, s, d = x.shape
    const = lambda shape: pl.BlockSpec(shape, lambda bi, si: (0,) * len(shape))
    head_spec = pl.BlockSpec((1, N_HEADS, tm, HEAD_PAD), lambda bi, si: (bi, 0, si, 0))
    head_shape = jax.ShapeDtypeStruct((b, N_HEADS, s, HEAD_PAD), BF16)
    return pl.pallas_call(
        _proj_kernel,
        grid=(b, s // tm),
        in_specs=[
            pl.BlockSpec((1, tm, d), lambda bi, si: (bi, si, 0)),
            const(g.shape), const(w1.shape), const(d64.shape), const(qg.shape), const(wq.shape),
            const(kvg.shape), const(wkv.shape),
            pl.BlockSpec((tm, LANES), lambda bi, si: (si, 0)),
        ],
        out_specs=[
            pl.BlockSpec((1, 2, tm, F_WIDTH), lambda bi, si: (bi, 0, si, 0)),
            head_spec, head_spec, head_spec,
        ],
        out_shape=[jax.ShapeDtypeStruct((b, 2, s, F_WIDTH), BF16), head_shape, head_shape, head_shape],
        compiler_params=pltpu.CompilerParams(
            dimension_semantics=("arbitrary", "arbitrary"), vmem_limit_bytes=_vmem_limit(48 << 20)),
        name="proj",
    )(x, g, w1, d64, qg, wq, kvg, wkv, cs)


def _fft1_kernel(g1_ref, z_ref, v_ref):
    v_ref[0] = _dot(g1_ref[...], z_ref[0]).astype(BF16)


def _fft1(g1, z, *, tc):
    b, r, c = z.shape
    return pl.pallas_call(
        _fft1_kernel,
        grid=(b, c // tc),
        in_specs=[pl.BlockSpec(g1.shape, lambda bi, ci: (0, 0)),
                  pl.BlockSpec((1, r, tc), lambda bi, ci: (bi, 0, ci))],
        out_specs=pl.BlockSpec((1, r, tc), lambda bi, ci: (bi, 0, ci)),
        out_shape=jax.ShapeDtypeStruct(z.shape, BF16),
        compiler_params=pltpu.CompilerParams(
            dimension_semantics=("arbitrary", "arbitrary"), vmem_limit_bytes=_vmem_limit(32 << 20)),
        name="fft1",
    )(g1, z)


def _fft2_kernel(mc_ref, ms_ref, vr_ref, vi_ref, y_ref):
    for j in range(mc_ref.shape[0]):
        y = _dot(mc_ref[j], vr_ref[0, 0, j]) + _dot(ms_ref[j], vi_ref[0, 0, j])
        y_ref[0, :, j * F_WIDTH:(j + 1) * F_WIDTH] = y.astype(BF16)


def _fft2(mc, ms, v, *, kb):
    b, _, s2, s1, f = v.shape
    m_spec = pl.BlockSpec((kb, s1, s1), lambda ki, bi: (ki, 0, 0))
    return pl.pallas_call(
        _fft2_kernel,
        grid=(s2 // kb, b),
        in_specs=[m_spec, m_spec,
                  pl.BlockSpec((1, 1, kb, s1, f), lambda ki, bi: (bi, 0, ki, 0, 0)),
                  pl.BlockSpec((1, 1, kb, s1, f), lambda ki, bi: (bi, 1, ki, 0, 0))],
        out_specs=pl.BlockSpec((1, s1, kb * f), lambda ki, bi: (bi, 0, ki)),
        out_shape=jax.ShapeDtypeStruct((b, s1, s2 * f), BF16),
        compiler_params=pltpu.CompilerParams(
            dimension_semantics=("arbitrary", "arbitrary"), vmem_limit_bytes=_vmem_limit(32 << 20)),
        name="fft2",
    )(mc, ms, v, v)


def _attn_kernel(q_ref, k_ref, v_ref, o_ref, m_s, acc_s, *, tq, tk):
    nq = q_ref.shape[2] // tq
    nk = k_ref.shape[2] // tk

    def q_body(qi, carry):
        qs = pl.multiple_of(qi * tq, tq)
        q = q_ref[0, 0, pl.ds(qs, tq), :]
        m_s[...] = jnp.full(m_s.shape, -1e30, F32)
        acc_s[...] = jnp.zeros(acc_s.shape, F32)

        def kv_body(kj, c):
            ks = pl.multiple_of(kj * tk, tk)
            kk = k_ref[0, 0, pl.ds(ks, tk), :]
            vv = v_ref[0, 0, pl.ds(ks, tk), :]
            s = lax.dot_general(q, kk, (((1,), (1,)), ((), ())), preferred_element_type=F32)
            m_prev = m_s[...]
            m_new = jnp.maximum(m_prev, jnp.max(s, axis=1, keepdims=True))
            alpha = jnp.exp2(m_prev - m_new)
            p = jnp.exp2(s - m_new).astype(BF16)
            acc_s[...] = alpha * acc_s[...] + _dot(p, vv)
            m_s[...] = m_new
            return c

        lax.fori_loop(0, nk, kv_body, 0)
        acc = acc_s[...]
        o_ref[0, pl.ds(qs, tq), :] = (acc[:, :V_HEAD_DIM] / acc[:, V_HEAD_DIM:]).astype(BF16)
        return carry

    lax.fori_loop(0, nq, q_body, 0)


def _attention(q, k, v, *, tq_step, tq, tk):
    b, h, s, w = q.shape
    kv_spec = pl.BlockSpec((1, 1, s, w), lambda bi, hi, qi: (bi, hi, 0, 0))
    return pl.pallas_call(
        functools.partial(_attn_kernel, tq=tq, tk=tk),
        grid=(b, h, s // tq_step),
        in_specs=[pl.BlockSpec((1, 1, tq_step, w), lambda bi, hi, qi: (bi, hi, qi, 0)), kv_spec, kv_spec],
        out_specs=pl.BlockSpec((1, tq_step, V_HEAD_DIM), lambda bi, hi, qi: (bi, qi, hi)),
        out_shape=jax.ShapeDtypeStruct((b, s, h * V_HEAD_DIM), BF16),
        scratch_shapes=[pltpu.VMEM((tq, 1), F32), pltpu.VMEM((tq, w), F32)],
        compiler_params=pltpu.CompilerParams(
            dimension_semantics=("arbitrary", "arbitrary", "arbitrary"),
            vmem_limit_bytes=_vmem_limit(48 << 20)),
        name="attention",
    )(q, k, v)


def _ffn_kernel(x_ref, yf_ref, ya_ref, wof_ref, woa_ref, g2_ref, wg_ref, wu_ref, wd_ref, gf_ref,
                o_ref, x1_s, xn_s, acc_s):
    f = pl.program_id(1)
    nf = pl.num_programs(1)

    @pl.when(f == 0)
    def _():
        x1 = x_ref[...] + _dot(yf_ref[...], wof_ref[...]) + _dot(ya_ref[...], woa_ref[...])
        x1_s[...] = x1
        xn_s[...] = _rms(x1, g2_ref[...]).astype(BF16)

    xn = xn_s[...]
    gate = _dot(xn, wg_ref[...])
    up = _dot(xn, wu_ref[...])
    hmid = (gate * jax.nn.sigmoid(gate) * up).astype(BF16)
    part = _dot(hmid, wd_ref[...])

    @pl.when(f == 0)
    def _():
        acc_s[...] = part

    @pl.when(f > 0)
    def _():
        acc_s[...] += part

    @pl.when(f == nf - 1)
    def _():
        o_ref[...] = _rms(x1_s[...] + acc_s[...], gf_ref[...])


def _ffn(x, yf, ya, wof, woa, g2, wg, wu, wd, gf, *, tm, tf):
    t, d = x.shape
    dff = wg.shape[1]
    row = lambda width: pl.BlockSpec((tm, width), lambda ti, fi: (ti, 0))
    const = lambda shape: pl.BlockSpec(shape, lambda ti, fi: (0,) * len(shape))
    return pl.pallas_call(
        _ffn_kernel,
        grid=(t // tm, dff // tf),
        in_specs=[row(d), row(yf.shape[1]), row(ya.shape[1]), const(wof.shape), const(woa.shape),
                  const(g2.shape),
                  pl.BlockSpec((d, tf), lambda ti, fi: (0, fi)),
                  pl.BlockSpec((d, tf), lambda ti, fi: (0, fi)),
                  pl.BlockSpec((tf, d), lambda ti, fi: (fi, 0)),
                  const(gf.shape)],
        out_specs=row(d),
        out_shape=jax.ShapeDtypeStruct((t, d), F32),
        scratch_shapes=[pltpu.VMEM((tm, d), F32), pltpu.VMEM((tm, d), BF16), pltpu.VMEM((tm, d), F32)],
        compiler_params=pltpu.CompilerParams(
            dimension_semantics=("arbitrary", "arbitrary"), vmem_limit_bytes=_vmem_limit(56 << 20)),
        name="ffn",
    )(x, yf, ya, wof, woa, g2, wg, wu, wd, gf)


def _swap_halves(w):
    half = w.shape[-1] // 2
    return jnp.concatenate([-w[..., half:], w[..., :half]], axis=-1)


def _rope_table(seq_len):
    inv_freq = 1.0 / (ROPE_THETA ** (jnp.arange(0, QK_ROPE_DIM, 2, dtype=F32) / QK_ROPE_DIM))
    ang = jnp.arange(seq_len, dtype=F32)[:, None] * inv_freq[None, :]
    cos, sin = jnp.cos(ang), jnp.sin(ang)
    return jnp.concatenate([cos, cos, sin, sin], axis=-1)


def _cos_sin(num, den):
    ang = (2.0 * math.pi / den) * (num % den).astype(F32)
    return jnp.cos(ang), jnp.sin(ang)


def _channel_dft():
    j = jnp.arange(FGROUP_DIM, dtype=jnp.int32)
    c, s = _cos_sin(j[:, None] * j[None, :], FGROUP_DIM)
    eye = jnp.eye(N_FGROUPS, dtype=F32)
    scale = 1.0 / math.sqrt(FGROUP_DIM)
    return jnp.concatenate([jnp.kron(eye, c * scale), jnp.kron(eye, -s * scale)], axis=1).astype(BF16)


def _seq_dft_tables(seq_len):
    s1 = FFT_S1
    s2 = seq_len // s1
    i2 = jnp.arange(s2, dtype=jnp.int32)
    c2, sn2 = _cos_sin(i2[:, None] * i2[None, :], s2)
    sc2 = 1.0 / math.sqrt(s2)
    g1 = jnp.concatenate([jnp.concatenate([c2, sn2], axis=1),
                          jnp.concatenate([-sn2, c2], axis=1)], axis=0) * sc2
    i1 = jnp.arange(s1, dtype=jnp.int32)
    ca, sa = _cos_sin(i1[:, None] * i1[None, :], s1)
    cb, sb = _cos_sin(i2[:, None] * i1[None, :], seq_len)
    sc1 = 1.0 / math.sqrt(s1)
    mc = (ca[None] * cb[:, None, :] - sa[None] * sb[:, None, :]) * sc1
    ms = (sa[None] * cb[:, None, :] + ca[None] * sb[:, None, :]) * sc1
    return g1.astype(BF16), mc.astype(BF16), ms.astype(BF16)


def _prep_weights(norm_mix_g, w_in, q_norm_g, w_q_up, kv_norm_g, w_kv_up, w_out, norm_ffn_g,
                  w_gate, w_up, w_down, final_norm_g):
    q_rank = q_norm_g.shape[-1]
    kv_rank = kv_norm_g.shape[-1]
    o3 = F_WIDTH + q_rank + kv_rank
    w1 = jnp.concatenate([w_in, _swap_halves(w_in[:, o3:])], axis=1).astype(BF16)
    wq = w_q_up.reshape(q_rank, N_HEADS, QK_HEAD_DIM)
    wq_rope = wq[..., QK_NOPE_DIM:]
    wq = jnp.concatenate([wq, _swap_halves(wq_rope)], axis=-1).reshape(q_rank, N_HEADS * HEAD_PAD)
    wkv = w_kv_up.reshape(kv_rank, N_HEADS, QK_NOPE_DIM + V_HEAD_DIM)
    wkv = jnp.concatenate([wkv[..., :QK_NOPE_DIM].reshape(kv_rank, -1),
                           wkv[..., QK_NOPE_DIM:].reshape(kv_rank, -1)], axis=1)
    return dict(
        g1=norm_mix_g[None, :], w1=w1, d64=_channel_dft(), qg=q_norm_g[None, :], wq=wq.astype(BF16),
        kvg=kv_norm_g[None, :], wkv=wkv.astype(BF16),
        wof=w_out[:F_WIDTH].astype(BF16), woa=w_out[F_WIDTH:].astype(BF16), g2=norm_ffn_g[None, :],
        wg=w_gate.astype(BF16), wu=w_up.astype(BF16), wd=w_down.astype(BF16), gf=final_norm_g[None, :])


def _trunk(x, w):
    b, s, d = x.shape
    s2 = s // FFT_S1
    g1, mc, ms = _seq_dft_tables(s)
    z, q, k, v = _proj(x, w["g1"], w["w1"], w["d64"], w["qg"], w["wq"], w["kvg"], w["wkv"],
                       _rope_table(s), tm=512)
    vz = _fft1(g1, z.reshape(b, 2 * s2, FFT_S1 * F_WIDTH), tc=8192)
    yf = _fft2(mc, ms, vz.reshape(b, 2, s2, FFT_S1, F_WIDTH), kb=8)
    ya = _attention(q, k, v, tq_step=2048, tq=512, tk=1024)
    out = _ffn(x.reshape(b * s, d), yf.reshape(b * s, F_WIDTH), ya.reshape(b * s, ATTN_WIDTH),
               w["wof"], w["woa"], w["g2"], w["wg"], w["wu"], w["wd"], w["gf"], tm=512, tf=1408)
    return out.reshape(b, s, d)


def kernel(x_prompt, x_sample, norm_mix_g, w_in, q_norm_g, w_q_up, kv_norm_g, w_kv_up, w_out, norm_ffn_g,
           w_gate, w_up, w_down, final_norm_g):
    assert w_in.shape[0] == 1, "single-layer trunk only"
    w = _prep_weights(norm_mix_g[0], w_in[0], q_norm_g[0], w_q_up[0], kv_norm_g[0], w_kv_up[0], w_out[0],
                      norm_ffn_g[0], w_gate[0], w_up[0], w_down[0], final_norm_g)
    return (_trunk(x_prompt, w), _trunk(x_sample, w))
```

```python
import functools
import math

import jax
import jax.numpy as jnp
from jax import lax
from jax.experimental import pallas as pl
from jax.experimental.pallas import tpu as pltpu

LANES = 128
MXU_DIM = 256
VMEM_BYTES = 64 * 1024 * 1024

N_FGROUPS = 4
FGROUP_DIM = 64
F_WIDTH = N_FGROUPS * FGROUP_DIM
N_HEADS = 6
QK_NOPE_DIM = 128
QK_ROPE_DIM = 64
V_HEAD_DIM = 128
QK_HEAD_DIM = QK_NOPE_DIM + QK_ROPE_DIM
ATTN_WIDTH = N_HEADS * V_HEAD_DIM
ROPE_THETA = 10000.0
EPS = 1e-6
SM_SCALE = 1.0 / math.sqrt(QK_HEAD_DIM)
LOG2E = 1.4426950408889634

HEAD_PAD = 2 * LANES
FFT_S1 = MXU_DIM

BF16 = jnp.bfloat16
F32 = jnp.float32


def _vmem_limit(nbytes):
    return int(min(nbytes, VMEM_BYTES - 4 * 1024 * 1024))


def _rms(x, g):
    return x * lax.rsqrt(jnp.mean(x * x, axis=-1, keepdims=True) + EPS) * g


def _dot(a, b):
    return jnp.dot(a, b, preferred_element_type=F32)


def _proj_kernel(x_ref, g_ref, w1_ref, d64_ref, qg_ref, wq_ref, kvg_ref, wkv_ref, cs_ref,
                 z_ref, q_ref, k_ref, v_ref):
    o1 = F_WIDTH
    o2 = o1 + qg_ref.shape[1]
    o3 = o2 + kvg_ref.shape[1]
    xn = _rms(x_ref[0], g_ref[...]).astype(BF16)
    h = _dot(xn, w1_ref[...])

    z = _dot(h[:, :o1].astype(BF16), d64_ref[...])
    z_ref[0, 0] = z[:, :o1].astype(BF16)
    z_ref[0, 1] = z[:, o1:].astype(BF16)

    cs = cs_ref[...]
    cqn = (_rms(h[:, o1:o2], qg_ref[...]) * (SM_SCALE * LOG2E)).astype(BF16)
    qf = _dot(cqn, wq_ref[...])
    for hh in range(N_HEADS):
        base = hh * HEAD_PAD
        q_ref[0, hh, :, :LANES] = qf[:, base:base + LANES].astype(BF16)
        q_ref[0, hh, :, LANES:] = (qf[:, base + LANES:base + HEAD_PAD] * cs).astype(BF16)

    ckvn = _rms(h[:, o2:o3], kvg_ref[...]).astype(BF16)
    kv = _dot(ckvn, wkv_ref[...])
    pk = h[:, o3:] * cs
    kr = (pk + pltpu.roll(pk, QK_ROPE_DIM, axis=1)).astype(BF16)
    ones = jnp.ones((kv.shape[0], LANES), BF16)
    for hh in range(N_HEADS):
        k_ref[0, hh, :, :LANES] = kv[:, hh * LANES:(hh + 1) * LANES].astype(BF16)
        k_ref[0, hh, :, LANES:] = kr
        v_ref[0, hh, :, :LANES] = kv[:, ATTN_WIDTH + hh * LANES:ATTN_WIDTH + (hh + 1) * LANES].astype(BF16)
        v_ref[0, hh, :, LANES:] = ones


def _proj(x, g, w1, d64, qg, wq, kvg, wkv, cs, *, tm):
    b, s, d = x.shape
    const = lambda shape: pl.BlockSpec(shape, lambda bi, si: (0,) * len(shape))
    head_spec = pl.BlockSpec((1, N_HEADS, tm, HEAD_PAD), lambda bi, si: (bi, 0, si, 0))
    head_shape = jax.ShapeDtypeStruct((b, N_HEADS, s, HEAD_PAD), BF16)
    return pl.pallas_call(
        _proj_kernel,
        grid=(b, s // tm),
        in_specs=[
            pl.BlockSpec((1, tm, d), lambda bi, si: (bi, si, 0)),
            const(g.shape), const(w1.shape), const(d64.shape), const(qg.shape), const(wq.shape),
            const(kvg.shape), const(wkv.shape),
            pl.BlockSpec((tm, LANES), lambda bi, si: (si, 0)),
        ],
        out_specs=[
            pl.BlockSpec((1, 2, tm, F_WIDTH), lambda bi, si: (bi, 0, si, 0)),
            head_spec, head_spec, head_spec,
        ],
        out_shape=[jax.ShapeDtypeStruct((b, 2, s, F_WIDTH), BF16), head_shape, head_shape, head_shape],
        compiler_params=pltpu.CompilerParams(
            dimension_semantics=("arbitrary", "arbitrary"), vmem_limit_bytes=_vmem_limit(48 << 20)),
        name="proj",
    )(x, g, w1, d64, qg, wq, kvg, wkv, cs)


def _fft1_kernel(g1_ref, z_ref, v_ref):
    v_ref[0] = _dot(g1_ref[...], z_ref[0]).astype(BF16)


def _fft1(g1, z, *, tc):
    b, r, c = z.shape
    return pl.pallas_call(
        _fft1_kernel,
        grid=(b, c // tc),
        in_specs=[pl.BlockSpec(g1.shape, lambda bi, ci: (0, 0)),
                  pl.BlockSpec((1, r, tc), lambda bi, ci: (bi, 0, ci))],
        out_specs=pl.BlockSpec((1, r, tc), lambda bi, ci: (bi, 0, ci)),
        out_shape=jax.ShapeDtypeStruct(z.shape, BF16),
        compiler_params=pltpu.CompilerParams(
            dimension_semantics=("arbitrary", "arbitrary"), vmem_limit_bytes=_vmem_limit(32 << 20)),
        name="fft1",
    )(g1, z)


def _fft2_kernel(mc_ref, ms_ref, vr_ref, vi_ref, y_ref):
    for j in range(mc_ref.shape[0]):
        y = _dot(mc_ref[j], vr_ref[0, 0, j]) + _dot(ms_ref[j], vi_ref[0, 0, j])
        y_ref[0, :, j * F_WIDTH:(j + 1) * F_WIDTH] = y.astype(BF16)


def _fft2(mc, ms, v, *, kb):
    b, _, s2, s1, f = v.shape
    m_spec = pl.BlockSpec((kb, s1, s1), lambda ki, bi: (ki, 0, 0))
    return pl.pallas_call(
        _fft2_kernel,
        grid=(s2 // kb, b),
        in_specs=[m_spec, m_spec,
                  pl.BlockSpec((1, 1, kb, s1, f), lambda ki, bi: (bi, 0, ki, 0, 0)),
                  pl.BlockSpec((1, 1, kb, s1, f), lambda ki, bi: (bi, 1, ki, 0, 0))],
        out_specs=pl.BlockSpec((1, s1, kb * f), lambda ki, bi: (bi, 0, ki)),
        out_shape=jax.ShapeDtypeStruct((b, s1, s2 * f), BF16),
        compiler_params=pltpu.CompilerParams(
            dimension_semantics=("arbitrary", "arbitrary"), vmem_limit_bytes=_vmem_limit(32 << 20)),
        name="fft2",
    )(mc, ms, v, v)


def _attn_kernel(q_ref, k_ref, v_ref, o_ref, s_s, m_s, acc_s, *, tq, tk):
    nq = q_ref.shape[2] // tq
    nk = k_ref.shape[2] // tk
    nsteps = nq * nk
    assert nk % 2 == 0

    def scores(t, slot):
        qs = pl.multiple_of(lax.div(t, nk) * tq, tq)
        ks = pl.multiple_of(lax.rem(t, nk) * tk, tk)
        s_s[slot] = lax.dot_general(q_ref[0, 0, pl.ds(qs, tq), :], k_ref[0, 0, pl.ds(ks, tk), :],
                                    (((1,), (1,)), ((), ())), preferred_element_type=F32)

    def softmax_pv(t, slot):
        kj = lax.rem(t, nk)
        ks = pl.multiple_of(kj * tk, tk)
        s = s_s[slot]
        m_prev = jnp.where(kj == 0, -1e30, m_s[...])
        m_new = jnp.maximum(m_prev, jnp.max(s, axis=1, keepdims=True))
        alpha = jnp.exp2(m_prev - m_new)
        p = jnp.exp2(s - m_new).astype(BF16)
        acc_s[...] = alpha * acc_s[...] + _dot(p, v_ref[0, 0, pl.ds(ks, tk), :])
        m_s[...] = m_new

    def finalize(t):
        qs = pl.multiple_of(lax.div(t, nk) * tq, tq)
        acc = acc_s[...]
        o_ref[0, pl.ds(qs, tq), :] = (acc[:, :V_HEAD_DIM] / acc[:, V_HEAD_DIM:]).astype(BF16)

    acc_s[...] = jnp.zeros(acc_s.shape, F32)
    scores(jnp.int32(0), 0)

    def body(u, carry):
        t = 2 * u
        scores(t + 1, 1)
        softmax_pv(t, 0)
        scores(t + 2, 0)
        softmax_pv(t + 1, 1)
        pl.when(lax.rem(t + 1, nk) == nk - 1)(functools.partial(finalize, t))
        return carry

    lax.fori_loop(0, nsteps // 2 - 1, body, 0)
    t_last = jnp.int32(nsteps - 2)
    scores(t_last + 1, 1)
    softmax_pv(t_last, 0)
    softmax_pv(t_last + 1, 1)
    finalize(t_last)


def _attention(q, k, v, *, tq_step, tq, tk):
    b, h, s, w = q.shape
    kv_spec = pl.BlockSpec((1, 1, s, w), lambda bi, hi, qi: (bi, hi, 0, 0))
    return pl.pallas_call(
        functools.partial(_attn_kernel, tq=tq, tk=tk),
        grid=(b, h, s // tq_step),
        in_specs=[pl.BlockSpec((1, 1, tq_step, w), lambda bi, hi, qi: (bi, hi, qi, 0)), kv_spec, kv_spec],
        out_specs=pl.BlockSpec((1, tq_step, V_HEAD_DIM), lambda bi, hi, qi: (bi, qi, hi)),
        out_shape=jax.ShapeDtypeStruct((b, s, h * V_HEAD_DIM), BF16),
        scratch_shapes=[pltpu.VMEM((2, tq, tk), F32), pltpu.VMEM((tq, 1), F32), pltpu.VMEM((tq, w), F32)],
        compiler_params=pltpu.CompilerParams(
            dimension_semantics=("arbitrary", "arbitrary", "arbitrary"),
            vmem_limit_bytes=_vmem_limit(48 << 20)),
        name="attention",
    )(q, k, v)


def _ffn_kernel(x_ref, yf_ref, ya_ref, wof_ref, woa_ref, g2_ref, wg_ref, wu_ref, wd_ref, gf_ref,
                o_ref, x1_s, xn_s, acc_s):
    f = pl.program_id(1)
    nf = pl.num_programs(1)

    @pl.when(f == 0)
    def _():
        x1 = x_ref[...] + _dot(yf_ref[...], wof_ref[...]) + _dot(ya_ref[...], woa_ref[...])
        x1_s[...] = x1
        xn_s[...] = _rms(x1, g2_ref[...]).astype(BF16)

    xn = xn_s[...]
    gate = _dot(xn, wg_ref[...])
    up = _dot(xn, wu_ref[...])
    hmid = (gate * jax.nn.sigmoid(gate) * up).astype(BF16)
    part = _dot(hmid, wd_ref[...])

    @pl.when(f == 0)
    def _():
        acc_s[...] = part

    @pl.when(f > 0)
    def _():
        acc_s[...] += part

    @pl.when(f == nf - 1)
    def _():
        o_ref[...] = _rms(x1_s[...] + acc_s[...], gf_ref[...])


def _ffn(x, yf, ya, wof, woa, g2, wg, wu, wd, gf, *, tm, tf):
    t, d = x.shape
    dff = wg.shape[1]
    row = lambda width: pl.BlockSpec((tm, width), lambda ti, fi: (ti, 0))
    const = lambda shape: pl.BlockSpec(shape, lambda ti, fi: (0,) * len(shape))
    return pl.pallas_call(
        _ffn_kernel,
        grid=(t // tm, dff // tf),
        in_specs=[row(d), row(yf.shape[1]), row(ya.shape[1]), const(wof.shape), const(woa.shape),
                  const(g2.shape),
                  pl.BlockSpec((d, tf), lambda ti, fi: (0, fi)),
                  pl.BlockSpec((d, tf), lambda ti, fi: (0, fi)),
                  pl.BlockSpec((tf, d), lambda ti, fi: (fi, 0)),
                  const(gf.shape)],
        out_specs=row(d),
        out_shape=jax.ShapeDtypeStruct((t, d), F32),
        scratch_shapes=[pltpu.VMEM((tm, d), F32), pltpu.VMEM((tm, d), BF16), pltpu.VMEM((tm, d), F32)],
        compiler_params=pltpu.CompilerParams(
            dimension_semantics=("arbitrary", "arbitrary"), vmem_limit_bytes=_vmem_limit(56 << 20)),
        name="ffn",
    )(x, yf, ya, wof, woa, g2, wg, wu, wd, gf)


def _swap_halves(w):
    half = w.shape[-1] // 2
    return jnp.concatenate([-w[..., half:], w[..., :half]], axis=-1)


def _rope_table(seq_len):
    inv_freq = 1.0 / (ROPE_THETA ** (jnp.arange(0, QK_ROPE_DIM, 2, dtype=F32) / QK_ROPE_DIM))
    ang = jnp.arange(seq_len, dtype=F32)[:, None] * inv_freq[None, :]
    cos, sin = jnp.cos(ang), jnp.sin(ang)
    return jnp.concatenate([cos, cos, sin, sin], axis=-1)


def _cos_sin(num, den):
    ang = (2.0 * math.pi / den) * (num % den).astype(F32)
    return jnp.cos(ang), jnp.sin(ang)


def _channel_dft():
    j = jnp.arange(FGROUP_DIM, dtype=jnp.int32)
    c, s = _cos_sin(j[:, None] * j[None, :], FGROUP_DIM)
    eye = jnp.eye(N_FGROUPS, dtype=F32)
    scale = 1.0 / math.sqrt(FGROUP_DIM)
    return jnp.concatenate([jnp.kron(eye, c * scale), jnp.kron(eye, -s * scale)], axis=1).astype(BF16)


def _seq_dft_tables(seq_len):
    s1 = FFT_S1
    s2 = seq_len // s1
    i2 = jnp.arange(s2, dtype=jnp.int32)
    c2, sn2 = _cos_sin(i2[:, None] * i2[None, :], s2)
    sc2 = 1.0 / math.sqrt(s2)
    g1 = jnp.concatenate([jnp.concatenate([c2, sn2], axis=1),
                          jnp.concatenate([-sn2, c2], axis=1)], axis=0) * sc2
    i1 = jnp.arange(s1, dtype=jnp.int32)
    ca, sa = _cos_sin(i1[:, None] * i1[None, :], s1)
    cb, sb = _cos_sin(i2[:, None] * i1[None, :], seq_len)
    sc1 = 1.0 / math.sqrt(s1)
    mc = (ca[None] * cb[:, None, :] - sa[None] * sb[:, None, :]) * sc1
    ms = (sa[None] * cb[:, None, :] + ca[None] * sb[:, None, :]) * sc1
    return g1.astype(BF16), mc.astype(BF16), ms.astype(BF16)


def _prep_weights(norm_mix_g, w_in, q_norm_g, w_q_up, kv_norm_g, w_kv_up, w_out, norm_ffn_g,
                  w_gate, w_up, w_down, final_norm_g):
    q_rank = q_norm_g.shape[-1]
    kv_rank = kv_norm_g.shape[-1]
    o3 = F_WIDTH + q_rank + kv_rank
    w1 = jnp.concatenate([w_in, _swap_halves(w_in[:, o3:])], axis=1).astype(BF16)
    wq = w_q_up.reshape(q_rank, N_HEADS, QK_HEAD_DIM)
    wq_rope = wq[..., QK_NOPE_DIM:]
    wq = jnp.concatenate([wq, _swap_halves(wq_rope)], axis=-1).reshape(q_rank, N_HEADS * HEAD_PAD)
    wkv = w_kv_up.reshape(kv_rank, N_HEADS, QK_NOPE_DIM + V_HEAD_DIM)
    wkv = jnp.concatenate([wkv[..., :QK_NOPE_DIM].reshape(kv_rank, -1),
                           wkv[..., QK_NOPE_DIM:].reshape(kv_rank, -1)], axis=1)
    return dict(
        g1=norm_mix_g[None, :], w1=w1, d64=_channel_dft(), qg=q_norm_g[None, :], wq=wq.astype(BF16),
        kvg=kv_norm_g[None, :], wkv=wkv.astype(BF16),
        wof=w_out[:F_WIDTH].astype(BF16), woa=w_out[F_WIDTH:].astype(BF16), g2=norm_ffn_g[None, :],
        wg=w_gate.astype(BF16), wu=w_up.astype(BF16), wd=w_down.astype(BF16), gf=final_norm_g[None, :])


def _trunk(x, w):
    b, s, d = x.shape
    s2 = s // FFT_S1
    g1, mc, ms = _seq_dft_tables(s)
    z, q, k, v = _proj(x, w["g1"], w["w1"], w["d64"], w["qg"], w["wq"], w["kvg"], w["wkv"],
                       _rope_table(s), tm=512)
    vz = _fft1(g1, z.reshape(b, 2 * s2, FFT_S1 * F_WIDTH), tc=8192)
    yf = _fft2(mc, ms, vz.reshape(b, 2, s2, FFT_S1, F_WIDTH), kb=8)
    ya = _attention(q, k, v, tq_step=2048, tq=512, tk=1024)
    out = _ffn(x.reshape(b * s, d), yf.reshape(b * s, F_WIDTH), ya.reshape(b * s, ATTN_WIDTH),
               w["wof"], w["woa"], w["g2"], w["wg"], w["wu"], w["wd"], w["gf"], tm=512, tf=1408)
    return out.reshape(b, s, d)


def kernel(x_prompt, x_sample, norm_mix_g, w_in, q_norm_g, w_q_up, kv_norm_g, w_kv_up, w_out, norm_ffn_g,
           w_gate, w_up, w_down, final_norm_g):
    assert w_in.shape[0] == 1, "single-layer trunk only"
    w = _prep_weights(norm_mix_g[0], w_in[0], q_norm_g[0], w_q_up[0], kv_norm_g[0], w_kv_up[0], w_out[0],
                      norm_ffn_g[0], w_gate[0], w_up[0], w_down[0], final_norm_g)
    return (_trunk(x_prompt, w), _trunk(x_sample, w))
```

```python
import functools
import math

import jax
import jax.numpy as jnp
from jax import lax
from jax.experimental import pallas as pl
from jax.experimental.pallas import tpu as pltpu

LANES = 128
MXU_DIM = 256
VMEM_BYTES = 64 * 1024 * 1024

N_FGROUPS = 4
FGROUP_DIM = 64
F_WIDTH = N_FGROUPS * FGROUP_DIM
N_HEADS = 6
QK_NOPE_DIM = 128
QK_ROPE_DIM = 64
V_HEAD_DIM = 128
QK_HEAD_DIM = QK_NOPE_DIM + QK_ROPE_DIM
ATTN_WIDTH = N_HEADS * V_HEAD_DIM
ROPE_THETA = 10000.0
EPS = 1e-6
SM_SCALE = 1.0 / math.sqrt(QK_HEAD_DIM)
LOG2E = 1.4426950408889634

HEAD_PAD = 2 * LANES
FFT_S1 = MXU_DIM

BF16 = jnp.bfloat16
F32 = jnp.float32


def _vmem_limit(nbytes):
    return int(min(nbytes, VMEM_BYTES - 4 * 1024 * 1024))


def _rms(x, g):
    return x * lax.rsqrt(jnp.mean(x * x, axis=-1, keepdims=True) + EPS) * g


def _dot(a, b):
    return jnp.dot(a, b, preferred_element_type=F32)


def _proj_kernel(x_ref, g_ref, w1_ref, d64_ref, qg_ref, wq_ref, kvg_ref, wkv_ref, cs_ref,
                 z_ref, q_ref, k_ref, v_ref):
    o1 = F_WIDTH
    o2 = o1 + qg_ref.shape[1]
    o3 = o2 + kvg_ref.shape[1]
    xn = _rms(x_ref[0], g_ref[...]).astype(BF16)
    h = _dot(xn, w1_ref[...])

    z = _dot(h[:, :o1].astype(BF16), d64_ref[...])
    z_ref[0, 0] = z[:, :o1].astype(BF16)
    z_ref[0, 1] = z[:, o1:].astype(BF16)

    cs = cs_ref[...]
    cqn = (_rms(h[:, o1:o2], qg_ref[...]) * (SM_SCALE * LOG2E)).astype(BF16)
    qf = _dot(cqn, wq_ref[...])
    for hh in range(N_HEADS):
        base = hh * HEAD_PAD
        q_ref[0, hh, :, :LANES] = qf[:, base:base + LANES].astype(BF16)
        q_ref[0, hh, :, LANES:] = (qf[:, base + LANES:base + HEAD_PAD] * cs).astype(BF16)

    ckvn = _rms(h[:, o2:o3], kvg_ref[...]).astype(BF16)
    kv = _dot(ckvn, wkv_ref[...])
    pk = h[:, o3:] * cs
    kr = (pk + pltpu.roll(pk, QK_ROPE_DIM, axis=1)).astype(BF16)
    ones = jnp.ones((kv.shape[0], LANES), BF16)
    for hh in range(N_HEADS):
        k_ref[0, hh, :, :LANES] = kv[:, hh * LANES:(hh + 1) * LANES].astype(BF16)
        k_ref[0, hh, :, LANES:] = kr
        v_ref[0, hh, :, :LANES] = kv[:, ATTN_WIDTH + hh * LANES:ATTN_WIDTH + (hh + 1) * LANES].astype(BF16)
        v_ref[0, hh, :, LANES:] = ones


def _proj(x, g, w1, d64, qg, wq, kvg, wkv, cs, *, tm):
    b, s, d = x.shape
    const = lambda shape: pl.BlockSpec(shape, lambda bi, si: (0,) * len(shape))
    head_spec = pl.BlockSpec((1, N_HEADS, tm, HEAD_PAD), lambda bi, si: (bi, 0, si, 0))
    head_shape = jax.ShapeDtypeStruct((b, N_HEADS, s, HEAD_PAD), BF16)
    return pl.pallas_call(
        _proj_kernel,
        grid=(b, s // tm),
        in_specs=[
            pl.BlockSpec((1, tm, d), lambda bi, si: (bi, si, 0)),
            const(g.shape), const(w1.shape), const(d64.shape), const(qg.shape), const(wq.shape),
            const(kvg.shape), const(wkv.shape),
            pl.BlockSpec((tm, LANES), lambda bi, si: (si, 0)),
        ],
        out_specs=[
            pl.BlockSpec((1, 2, tm, F_WIDTH), lambda bi, si: (bi, 0, si, 0)),
            head_spec, head_spec, head_spec,
        ],
        out_shape=[jax.ShapeDtypeStruct((b, 2, s, F_WIDTH), BF16), head_shape, head_shape, head_shape],
        compiler_params=pltpu.CompilerParams(
            dimension_semantics=("arbitrary", "arbitrary"), vmem_limit_bytes=_vmem_limit(48 << 20)),
        name="proj",
    )(x, g, w1, d64, qg, wq, kvg, wkv, cs)


def _fft1_kernel(g1_ref, z_ref, v_ref):
    v_ref[0] = _dot(g1_ref[...], z_ref[0]).astype(BF16)


def _fft1(g1, z, *, tc):
    b, r, c = z.shape
    return pl.pallas_call(
        _fft1_kernel,
        grid=(b, c // tc),
        in_specs=[pl.BlockSpec(g1.shape, lambda bi, ci: (0, 0)),
                  pl.BlockSpec((1, r, tc), lambda bi, ci: (bi, 0, ci))],
        out_specs=pl.BlockSpec((1, r, tc), lambda bi, ci: (bi, 0, ci)),
        out_shape=jax.ShapeDtypeStruct(z.shape, BF16),
        compiler_params=pltpu.CompilerParams(
            dimension_semantics=("arbitrary", "arbitrary"), vmem_limit_bytes=_vmem_limit(32 << 20)),
        name="fft1",
    )(g1, z)


def _fft2_kernel(mc_ref, ms_ref, vr_ref, vi_ref, y_ref):
    for j in range(mc_ref.shape[0]):
        y = _dot(mc_ref[j], vr_ref[0, 0, j]) + _dot(ms_ref[j], vi_ref[0, 0, j])
        y_ref[0, :, j * F_WIDTH:(j + 1) * F_WIDTH] = y.astype(BF16)


def _fft2(mc, ms, v, *, kb):
    b, _, s2, s1, f = v.shape
    m_spec = pl.BlockSpec((kb, s1, s1), lambda ki, bi: (ki, 0, 0))
    return pl.pallas_call(
        _fft2_kernel,
        grid=(s2 // kb, b),
        in_specs=[m_spec, m_spec,
                  pl.BlockSpec((1, 1, kb, s1, f), lambda ki, bi: (bi, 0, ki, 0, 0)),
                  pl.BlockSpec((1, 1, kb, s1, f), lambda ki, bi: (bi, 1, ki, 0, 0))],
        out_specs=pl.BlockSpec((1, s1, kb * f), lambda ki, bi: (bi, 0, ki)),
        out_shape=jax.ShapeDtypeStruct((b, s1, s2 * f), BF16),
        compiler_params=pltpu.CompilerParams(
            dimension_semantics=("arbitrary", "arbitrary"), vmem_limit_bytes=_vmem_limit(32 << 20)),
        name="fft2",
    )(mc, ms, v, v)


ATTN_UNROLL = 4
ATTN_AHEAD = 2


def _attn_kernel(q_ref, k_ref, v_ref, o_ref, s_s, mx_s, m_s, acc_s, *, tq, tk):
    hb = q_ref.shape[1]
    nq = q_ref.shape[2] // tq
    nk = k_ref.shape[2] // tk
    nsteps = hb * nq * nk
    assert nsteps % ATTN_UNROLL == 0 and (nk == 1 or nk % ATTN_UNROLL == 0)

    def split(t):
        r = lax.div(t, nk)
        return lax.div(r, nq), pl.multiple_of(lax.rem(r, nq) * tq, tq), lax.rem(t, nk)

    def scores(t, slot):
        hh, qs, kj = split(t)
        ks = pl.multiple_of(kj * tk, tk)
        s = lax.dot_general(q_ref[0, hh, pl.ds(qs, tq), :], k_ref[0, hh, pl.ds(ks, tk), :],
                            (((1,), (1,)), ((), ())), preferred_element_type=F32)
        s_s[slot] = s
        pm = s[:, :LANES]
        for c in range(LANES, tk, LANES):
            pm = jnp.maximum(pm, s[:, c:c + LANES])
        mx_s[slot] = pm

    def normalized(acc):
        return (acc[:, :V_HEAD_DIM] / acc[:, V_HEAD_DIM:]).astype(BF16)

    def softmax_pv(t, slot):
        hh, qs, kj = split(t)
        vv = v_ref[0, hh, pl.ds(pl.multiple_of(kj * tk, tk), tk), :]
        mx = jnp.max(mx_s[slot], axis=1, keepdims=True)
        if nk == 1:
            p = jnp.exp2(s_s[slot] - mx).astype(BF16)
            o_ref[0, hh, pl.ds(qs, tq), :] = normalized(_dot(p, vv))
        else:
            m_prev = jnp.where(kj == 0, -1e30, m_s[...])
            m_new = jnp.maximum(m_prev, mx)
            alpha = jnp.exp2(m_prev - m_new)
            p = jnp.exp2(s_s[slot] - m_new).astype(BF16)
            acc_s[...] = alpha * acc_s[...] + _dot(p, vv)
            m_s[...] = m_new

    def finalize(t):
        hh, qs, _ = split(t)
        o_ref[0, hh, pl.ds(qs, tq), :] = normalized(acc_s[...])

    def steps(t0, last):
        for i in range(ATTN_UNROLL):
            if not (last and i + ATTN_AHEAD >= ATTN_UNROLL):
                scores(t0 + (i + ATTN_AHEAD), (i + ATTN_AHEAD) % ATTN_UNROLL)
            softmax_pv(t0 + i, i)
        t_end = t0 + (ATTN_UNROLL - 1)
        if nk > 1 and last:
            finalize(t_end)
        elif nk > 1:
            pl.when(lax.rem(t_end, nk) == nk - 1)(functools.partial(finalize, t_end))

    if nk > 1:
        acc_s[...] = jnp.zeros(acc_s.shape, F32)
    for i in range(ATTN_AHEAD):
        scores(jnp.int32(i), i)

    def body(u, carry):
        steps(u * ATTN_UNROLL, False)
        return carry

    lax.fori_loop(0, nsteps // ATTN_UNROLL - 1, body, 0)
    steps(jnp.int32(nsteps - ATTN_UNROLL), True)


def _attention(q, k, v, *, hb, tq, tk):
    b, h, s, w = q.shape
    spec = pl.BlockSpec((1, hb, s, w), lambda bi, hi: (bi, hi, 0, 0))
    return pl.pallas_call(
        functools.partial(_attn_kernel, tq=tq, tk=tk),
        grid=(b, h // hb),
        in_specs=[spec, spec, spec],
        out_specs=pl.BlockSpec((1, hb, s, V_HEAD_DIM), lambda bi, hi: (bi, hi, 0, 0)),
        out_shape=jax.ShapeDtypeStruct((b, h, s, V_HEAD_DIM), BF16),
        scratch_shapes=[pltpu.VMEM((ATTN_UNROLL, tq, tk), F32), pltpu.VMEM((ATTN_UNROLL, tq, LANES), F32),
                        pltpu.VMEM((tq, 1), F32), pltpu.VMEM((tq, w), F32)],
        compiler_params=pltpu.CompilerParams(
            dimension_semantics=("arbitrary", "arbitrary"), vmem_limit_bytes=_vmem_limit(56 << 20)),
        name="attention",
    )(q, k, v)


def _ffn_kernel(x_ref, yf_ref, ya_ref, wof_ref, woa_ref, g2_ref, wg_ref, wu_ref, wd_ref, gf_ref, o_ref, *, tf):
    ya = jnp.concatenate([ya_ref[0, hh] for hh in range(ya_ref.shape[1])], axis=1)
    x1 = x_ref[0] + _dot(yf_ref[0], wof_ref[...]) + _dot(ya, woa_ref[...])
    xn = _rms(x1, g2_ref[...]).astype(BF16)
    ffn = None
    for c in range(0, wg_ref.shape[1], tf):
        gate = _dot(xn, wg_ref[:, c:c + tf])
        up = _dot(xn, wu_ref[:, c:c + tf])
        hmid = (gate * jax.nn.sigmoid(gate) * up).astype(BF16)
        part = _dot(hmid, wd_ref[c:c + tf, :])
        ffn = part if ffn is None else ffn + part
    o_ref[0] = _rms(x1 + ffn, gf_ref[...])


def _ffn(x, yf, ya, wof, woa, g2, wg, wu, wd, gf, *, tm, tf):
    b, s, d = x.shape
    h = ya.shape[1]
    const = lambda shape: pl.BlockSpec(shape, lambda bi, si: (0,) * len(shape), pipeline_mode=pl.Buffered(1))
    row = lambda width: pl.BlockSpec((1, tm, width), lambda bi, si: (bi, si, 0))
    return pl.pallas_call(
        functools.partial(_ffn_kernel, tf=tf),
        grid=(b, s // tm),
        in_specs=[row(d), row(yf.shape[2]),
                  pl.BlockSpec((1, h, tm, V_HEAD_DIM), lambda bi, si: (bi, 0, si, 0)),
                  const(wof.shape), const(woa.shape), const(g2.shape), const(wg.shape), const(wu.shape),
                  const(wd.shape), const(gf.shape)],
        out_specs=row(d),
        out_shape=jax.ShapeDtypeStruct((b, s, d), F32),
        compiler_params=pltpu.CompilerParams(
            dimension_semantics=("arbitrary", "arbitrary"), vmem_limit_bytes=_vmem_limit(56 << 20)),
        name="ffn",
    )(x, yf, ya, wof, woa, g2, wg, wu, wd, gf)


def _swap_halves(w):
    half = w.shape[-1] // 2
    return jnp.concatenate([-w[..., half:], w[..., :half]], axis=-1)


def _rope_table(seq_len):
    inv_freq = 1.0 / (ROPE_THETA ** (jnp.arange(0, QK_ROPE_DIM, 2, dtype=F32) / QK_ROPE_DIM))
    ang = jnp.arange(seq_len, dtype=F32)[:, None] * inv_freq[None, :]
    cos, sin = jnp.cos(ang), jnp.sin(ang)
    return jnp.concatenate([cos, cos, sin, sin], axis=-1)


def _cos_sin(num, den):
    ang = (2.0 * math.pi / den) * (num % den).astype(F32)
    return jnp.cos(ang), jnp.sin(ang)


def _channel_dft():
    j = jnp.arange(FGROUP_DIM, dtype=jnp.int32)
    c, s = _cos_sin(j[:, None] * j[None, :], FGROUP_DIM)
    eye = jnp.eye(N_FGROUPS, dtype=F32)
    scale = 1.0 / math.sqrt(FGROUP_DIM)
    return jnp.concatenate([jnp.kron(eye, c * scale), jnp.kron(eye, -s * scale)], axis=1).astype(BF16)


def _seq_dft_tables(seq_len):
    s1 = FFT_S1
    s2 = seq_len // s1
    i2 = jnp.arange(s2, dtype=jnp.int32)
    c2, sn2 = _cos_sin(i2[:, None] * i2[None, :], s2)
    sc2 = 1.0 / math.sqrt(s2)
    g1 = jnp.concatenate([jnp.concatenate([c2, sn2], axis=1),
                          jnp.concatenate([-sn2, c2], axis=1)], axis=0) * sc2
    i1 = jnp.arange(s1, dtype=jnp.int32)
    ca, sa = _cos_sin(i1[:, None] * i1[None, :], s1)
    cb, sb = _cos_sin(i2[:, None] * i1[None, :], seq_len)
    sc1 = 1.0 / math.sqrt(s1)
    mc = (ca[None] * cb[:, None, :] - sa[None] * sb[:, None, :]) * sc1
    ms = (sa[None] * cb[:, None, :] + ca[None] * sb[:, None, :]) * sc1
    return g1.astype(BF16), mc.astype(BF16), ms.astype(BF16)


def _prep_weights(norm_mix_g, w_in, q_norm_g, w_q_up, kv_norm_g, w_kv_up, w_out, norm_ffn_g,
                  w_gate, w_up, w_down, final_norm_g):
    q_rank = q_norm_g.shape[-1]
    kv_rank = kv_norm_g.shape[-1]
    o3 = F_WIDTH + q_rank + kv_rank
    w1 = jnp.concatenate([w_in, _swap_halves(w_in[:, o3:])], axis=1).astype(BF16)
    wq = w_q_up.reshape(q_rank, N_HEADS, QK_HEAD_DIM)
    wq_rope = wq[..., QK_NOPE_DIM:]
    wq = jnp.concatenate([wq, _swap_halves(wq_rope)], axis=-1).reshape(q_rank, N_HEADS * HEAD_PAD)
    wkv = w_kv_up.reshape(kv_rank, N_HEADS, QK_NOPE_DIM + V_HEAD_DIM)
    wkv = jnp.concatenate([wkv[..., :QK_NOPE_DIM].reshape(kv_rank, -1),
                           wkv[..., QK_NOPE_DIM:].reshape(kv_rank, -1)], axis=1)
    return dict(
        g1=norm_mix_g[None, :], w1=w1, d64=_channel_dft(), qg=q_norm_g[None, :], wq=wq.astype(BF16),
        kvg=kv_norm_g[None, :], wkv=wkv.astype(BF16),
        wof=w_out[:F_WIDTH].astype(BF16), woa=w_out[F_WIDTH:].astype(BF16), g2=norm_ffn_g[None, :],
        wg=w_gate.astype(BF16), wu=w_up.astype(BF16), wd=w_down.astype(BF16), gf=final_norm_g[None, :])


def _trunk(x, w):
    b, s, d = x.shape
    s2 = s // FFT_S1
    g1, mc, ms = _seq_dft_tables(s)
    z, q, k, v = _proj(x, w["g1"], w["w1"], w["d64"], w["qg"], w["wq"], w["kvg"], w["wkv"],
                       _rope_table(s), tm=512)
    vz = _fft1(g1, z.reshape(b, 2 * s2, FFT_S1 * F_WIDTH), tc=8192)
    yf = _fft2(mc, ms, vz.reshape(b, 2, s2, FFT_S1, F_WIDTH), kb=8)
    attn_tiles = dict(hb=1, tq=512, tk=1024) if s > 2048 else dict(hb=3, tq=512, tk=s)
    ya = _attention(q, k, v, **attn_tiles)
    return _ffn(x, yf.reshape(b, s, F_WIDTH), ya, w["wof"], w["woa"], w["g2"], w["wg"], w["wu"], w["wd"],
                w["gf"], tm=512, tf=MXU_DIM)


def kernel(x_prompt, x_sample, norm_mix_g, w_in, q_norm_g, w_q_up, kv_norm_g, w_kv_up, w_out, norm_ffn_g,
           w_gate, w_up, w_down, final_norm_g):
    assert w_in.shape[0] == 1, "single-layer trunk only"
    w = _prep_weights(norm_mix_g[0], w_in[0], q_norm_g[0], w_q_up[0], kv_norm_g[0], w_kv_up[0], w_out[0],
                      norm_ffn_g[0], w_gate[0], w_up[0], w_down[0], final_norm_g)
    return (_trunk(x_prompt, w), _trunk(x_sample, w))
```

```python
import functools
import math

import jax
import jax.numpy as jnp
from jax import lax
from jax.experimental import pallas as pl
from jax.experimental.pallas import tpu as pltpu

LANES = 128
MXU_DIM = 256
VMEM_BYTES = 64 * 1024 * 1024

N_FGROUPS = 4
FGROUP_DIM = 64
F_WIDTH = N_FGROUPS * FGROUP_DIM
N_HEADS = 6
QK_NOPE_DIM = 128
QK_ROPE_DIM = 64
V_HEAD_DIM = 128
QK_HEAD_DIM = QK_NOPE_DIM + QK_ROPE_DIM
ATTN_WIDTH = N_HEADS * V_HEAD_DIM
ROPE_THETA = 10000.0
EPS = 1e-6
SM_SCALE = 1.0 / math.sqrt(QK_HEAD_DIM)
LOG2E = 1.4426950408889634

HEAD_PAD = 2 * LANES
FFT_S1 = MXU_DIM

BF16 = jnp.bfloat16
F32 = jnp.float32


def _vmem_limit(nbytes):
    return int(min(nbytes, VMEM_BYTES - 4 * 1024 * 1024))


def _rms(x, g):
    return x * lax.rsqrt(jnp.mean(x * x, axis=-1, keepdims=True) + EPS) * g


def _dot(a, b):
    return jnp.dot(a, b, preferred_element_type=F32)


def _proj_kernel(x_ref, g_ref, w1_ref, d64_ref, kc_ref, ks_ref, qg_ref, wq_ref, kvg_ref, wkv_ref, cs_ref,
                 v1_ref, q_ref, k_ref, v_ref):
    s2, r, d = x_ref.shape[1:]
    tm = s2 * r
    o1 = F_WIDTH
    o2 = o1 + qg_ref.shape[1]
    o3 = o2 + kvg_ref.shape[1]
    xn = _rms(x_ref[0].reshape(tm, d), g_ref[...]).astype(BF16)
    h = _dot(xn, w1_ref[...])

    z = _dot(h[:, :o1].astype(BF16), d64_ref[...]).astype(BF16)
    v1 = (_dot(kc_ref[...], z[:, :2 * o1]) + _dot(ks_ref[...], z[:, 2 * o1:])).astype(BF16)
    v1_ref[0, 0] = v1[:, :o1].reshape(s2, r, o1)
    v1_ref[0, 1] = v1[:, o1:].reshape(s2, r, o1)

    cs = cs_ref[...].reshape(tm, LANES)
    cqn = (_rms(h[:, o1:o2], qg_ref[...]) * (SM_SCALE * LOG2E)).astype(BF16)
    qf = _dot(cqn, wq_ref[...])
    for hh in range(N_HEADS):
        base = hh * HEAD_PAD
        q_ref[0, hh, :, :, :LANES] = qf[:, base:base + LANES].astype(BF16).reshape(s2, r, LANES)
        q_ref[0, hh, :, :, LANES:] = (qf[:, base + LANES:base + HEAD_PAD] * cs).astype(BF16).reshape(s2, r, LANES)

    ckvn = _rms(h[:, o2:o3], kvg_ref[...]).astype(BF16)
    kv = _dot(ckvn, wkv_ref[...])
    pk = h[:, o3:] * cs
    kr = (pk + pltpu.roll(pk, QK_ROPE_DIM, axis=1)).astype(BF16).reshape(s2, r, LANES)
    ones = jnp.ones((s2, r, LANES), BF16)
    for hh in range(N_HEADS):
        k_ref[0, hh, :, :, :LANES] = kv[:, hh * LANES:(hh + 1) * LANES].astype(BF16).reshape(s2, r, LANES)
        k_ref[0, hh, :, :, LANES:] = kr
        vh = kv[:, ATTN_WIDTH + hh * LANES:ATTN_WIDTH + (hh + 1) * LANES]
        v_ref[0, hh, :, :, :LANES] = vh.astype(BF16).reshape(s2, r, LANES)
        v_ref[0, hh, :, :, LANES:] = ones


def _proj(x, g, w1, d64, kc, ks, qg, wq, kvg, wkv, cs, *, tm):
    b, s, d = x.shape
    s1 = FFT_S1
    s2 = s // s1
    r = tm // s2
    const = lambda shape: pl.BlockSpec(shape, lambda bi, ri: (0,) * len(shape))
    head_spec = pl.BlockSpec((1, N_HEADS, s2, r, HEAD_PAD), lambda bi, ri: (bi, 0, 0, ri, 0))
    head_shape = jax.ShapeDtypeStruct((b, N_HEADS, s2, s1, HEAD_PAD), BF16)
    v1, q, k, v = pl.pallas_call(
        _proj_kernel,
        grid=(b, s1 // r),
        in_specs=[
            pl.BlockSpec((1, s2, r, d), lambda bi, ri: (bi, 0, ri, 0)),
            const(g.shape), const(w1.shape), const(d64.shape), const(kc.shape), const(ks.shape),
            const(qg.shape), const(wq.shape), const(kvg.shape), const(wkv.shape),
            pl.BlockSpec((s2, r, LANES), lambda bi, ri: (0, ri, 0)),
        ],
        out_specs=[
            pl.BlockSpec((1, 2, s2, r, F_WIDTH), lambda bi, ri: (bi, 0, 0, ri, 0)),
            head_spec, head_spec, head_spec,
        ],
        out_shape=[jax.ShapeDtypeStruct((b, 2, s2, s1, F_WIDTH), BF16), head_shape, head_shape, head_shape],
        compiler_params=pltpu.CompilerParams(
            dimension_semantics=("arbitrary", "arbitrary"), vmem_limit_bytes=_vmem_limit(48 << 20)),
        name="proj",
    )(x.reshape(b, s2, s1, d), g, w1, d64, kc, ks, qg, wq, kvg, wkv, cs.reshape(s2, s1, LANES))
    merge = lambda a: a.reshape(b, N_HEADS, s, HEAD_PAD)
    return v1, merge(q), merge(k), merge(v)


def _fft2_kernel(mc_ref, ms_ref, vr_ref, vi_ref, y_ref):
    for j in range(mc_ref.shape[0]):
        y = _dot(mc_ref[j], vr_ref[0, 0, j]) + _dot(ms_ref[j], vi_ref[0, 0, j])
        y_ref[0, :, j * F_WIDTH:(j + 1) * F_WIDTH] = y.astype(BF16)


def _fft2(mc, ms, v, *, kb):
    b, _, s2, s1, f = v.shape
    m_spec = pl.BlockSpec((kb, s1, s1), lambda ki, bi: (ki, 0, 0))
    return pl.pallas_call(
        _fft2_kernel,
        grid=(s2 // kb, b),
        in_specs=[m_spec, m_spec,
                  pl.BlockSpec((1, 1, kb, s1, f), lambda ki, bi: (bi, 0, ki, 0, 0)),
                  pl.BlockSpec((1, 1, kb, s1, f), lambda ki, bi: (bi, 1, ki, 0, 0))],
        out_specs=pl.BlockSpec((1, s1, kb * f), lambda ki, bi: (bi, 0, ki)),
        out_shape=jax.ShapeDtypeStruct((b, s1, s2 * f), BF16),
        compiler_params=pltpu.CompilerParams(
            dimension_semantics=("arbitrary", "arbitrary"), vmem_limit_bytes=_vmem_limit(32 << 20)),
        name="fft2",
    )(mc, ms, v, v)


ATTN_UNROLL = 4
ATTN_AHEAD = 2


def _attn_kernel(q_ref, k_ref, v_ref, o_ref, s_s, mx_s, m_s, acc_s, *, tq, tk):
    hb = q_ref.shape[1]
    nq = q_ref.shape[2] // tq
    nk = k_ref.shape[2] // tk
    nsteps = hb * nq * nk
    assert nsteps % ATTN_UNROLL == 0 and (nk == 1 or nk % ATTN_UNROLL == 0)

    def split(t):
        r = lax.div(t, nk)
        return lax.div(r, nq), pl.multiple_of(lax.rem(r, nq) * tq, tq), lax.rem(t, nk)

    def scores(t, slot):
        hh, qs, kj = split(t)
        ks = pl.multiple_of(kj * tk, tk)
        s = lax.dot_general(q_ref[0, hh, pl.ds(qs, tq), :], k_ref[0, hh, pl.ds(ks, tk), :],
                            (((1,), (1,)), ((), ())), preferred_element_type=F32)
        s_s[slot] = s
        pm = s[:, :LANES]
        for c in range(LANES, tk, LANES):
            pm = jnp.maximum(pm, s[:, c:c + LANES])
        mx_s[slot] = pm

    def normalized(acc):
        return (acc[:, :V_HEAD_DIM] / acc[:, V_HEAD_DIM:]).astype(BF16)

    def softmax_pv(t, slot):
        hh, qs, kj = split(t)
        vv = v_ref[0, hh, pl.ds(pl.multiple_of(kj * tk, tk), tk), :]
        mx = jnp.max(mx_s[slot], axis=1, keepdims=True)
        if nk == 1:
            p = jnp.exp2(s_s[slot] - mx).astype(BF16)
            o_ref[0, hh, pl.ds(qs, tq), :] = normalized(_dot(p, vv))
        else:
            m_prev = jnp.where(kj == 0, -1e30, m_s[...])
            m_new = jnp.maximum(m_prev, mx)
            alpha = jnp.exp2(m_prev - m_new)
            p = jnp.exp2(s_s[slot] - m_new).astype(BF16)
            acc_s[...] = alpha * acc_s[...] + _dot(p, vv)
            m_s[...] = m_new

    def finalize(t):
        hh, qs, _ = split(t)
        o_ref[0, hh, pl.ds(qs, tq), :] = normalized(acc_s[...])

    def steps(t0, last):
        for i in range(ATTN_UNROLL):
            if not (last and i + ATTN_AHEAD >= ATTN_UNROLL):
                scores(t0 + (i + ATTN_AHEAD), (i + ATTN_AHEAD) % ATTN_UNROLL)
            softmax_pv(t0 + i, i)
        t_end = t0 + (ATTN_UNROLL - 1)
        if nk > 1 and last:
            finalize(t_end)
        elif nk > 1:
            pl.when(lax.rem(t_end, nk) == nk - 1)(functools.partial(finalize, t_end))

    if nk > 1:
        acc_s[...] = jnp.zeros(acc_s.shape, F32)
    for i in range(ATTN_AHEAD):
        scores(jnp.int32(i), i)

    def body(u, carry):
        steps(u * ATTN_UNROLL, False)
        return carry

    lax.fori_loop(0, nsteps // ATTN_UNROLL - 1, body, 0)
    steps(jnp.int32(nsteps - ATTN_UNROLL), True)


def _attention(q, k, v, *, hb, tq, tk):
    b, h, s, w = q.shape
    spec = pl.BlockSpec((1, hb, s, w), lambda bi, hi: (bi, hi, 0, 0))
    return pl.pallas_call(
        functools.partial(_attn_kernel, tq=tq, tk=tk),
        grid=(b, h // hb),
        in_specs=[spec, spec, spec],
        out_specs=pl.BlockSpec((1, hb, s, V_HEAD_DIM), lambda bi, hi: (bi, hi, 0, 0)),
        out_shape=jax.ShapeDtypeStruct((b, h, s, V_HEAD_DIM), BF16),
        scratch_shapes=[pltpu.VMEM((ATTN_UNROLL, tq, tk), F32), pltpu.VMEM((ATTN_UNROLL, tq, LANES), F32),
                        pltpu.VMEM((tq, 1), F32), pltpu.VMEM((tq, w), F32)],
        compiler_params=pltpu.CompilerParams(
            dimension_semantics=("arbitrary", "arbitrary"), vmem_limit_bytes=_vmem_limit(56 << 20)),
        name="attention",
    )(q, k, v)


def _ffn_kernel(x_ref, yf_ref, ya_ref, wof_ref, woa_ref, g2_ref, wg_ref, wu_ref, wd_ref, gf_ref, o_ref, *, tf):
    ya = jnp.concatenate([ya_ref[0, hh] for hh in range(ya_ref.shape[1])], axis=1)
    x1 = x_ref[0] + _dot(yf_ref[0], wof_ref[...]) + _dot(ya, woa_ref[...])
    xn = _rms(x1, g2_ref[...]).astype(BF16)
    ffn = None
    for c in range(0, wg_ref.shape[1], tf):
        gate = _dot(xn, wg_ref[:, c:c + tf])
        up = _dot(xn, wu_ref[:, c:c + tf])
        hmid = (gate * jax.nn.sigmoid(gate) * up).astype(BF16)
        part = _dot(hmid, wd_ref[c:c + tf, :])
        ffn = part if ffn is None else ffn + part
    o_ref[0] = _rms(x1 + ffn, gf_ref[...])


def _ffn(x, yf, ya, wof, woa, g2, wg, wu, wd, gf, *, tm, tf):
    b, s, d = x.shape
    h = ya.shape[1]
    const = lambda shape: pl.BlockSpec(shape, lambda bi, si: (0,) * len(shape), pipeline_mode=pl.Buffered(1))
    row = lambda width: pl.BlockSpec((1, tm, width), lambda bi, si: (bi, si, 0))
    return pl.pallas_call(
        functools.partial(_ffn_kernel, tf=tf),
        grid=(b, s // tm),
        in_specs=[row(d), row(yf.shape[2]),
                  pl.BlockSpec((1, h, tm, V_HEAD_DIM), lambda bi, si: (bi, 0, si, 0)),
                  const(wof.shape), const(woa.shape), const(g2.shape), const(wg.shape), const(wu.shape),
                  const(wd.shape), const(gf.shape)],
        out_specs=row(d),
        out_shape=jax.ShapeDtypeStruct((b, s, d), F32),
        compiler_params=pltpu.CompilerParams(
            dimension_semantics=("arbitrary", "arbitrary"), vmem_limit_bytes=_vmem_limit(56 << 20)),
        name="ffn",
    )(x, yf, ya, wof, woa, g2, wg, wu, wd, gf)


def _swap_halves(w):
    half = w.shape[-1] // 2
    return jnp.concatenate([-w[..., half:], w[..., :half]], axis=-1)


def _rope_table(seq_len):
    inv_freq = 1.0 / (ROPE_THETA ** (jnp.arange(0, QK_ROPE_DIM, 2, dtype=F32) / QK_ROPE_DIM))
    ang = jnp.arange(seq_len, dtype=F32)[:, None] * inv_freq[None, :]
    cos, sin = jnp.cos(ang), jnp.sin(ang)
    return jnp.concatenate([cos, cos, sin, sin], axis=-1)


def _cos_sin(num, den):
    ang = (2.0 * math.pi / den) * (num % den).astype(F32)
    return jnp.cos(ang), jnp.sin(ang)


def _channel_dft():
    j = jnp.arange(FGROUP_DIM, dtype=jnp.int32)
    c, s = _cos_sin(j[:, None] * j[None, :], FGROUP_DIM)
    eye = jnp.eye(N_FGROUPS, dtype=F32)
    scale = 1.0 / math.sqrt(FGROUP_DIM)
    re, im = jnp.kron(eye, c * scale), jnp.kron(eye, -s * scale)
    return jnp.concatenate([re, im, im, -re], axis=1).astype(BF16)


def _seq_dft_tables(seq_len, r):
    s1 = FFT_S1
    s2 = seq_len // s1
    i2 = jnp.arange(s2, dtype=jnp.int32)
    c2, sn2 = _cos_sin(i2[:, None] * i2[None, :], s2)
    sc2 = 1.0 / math.sqrt(s2)
    eye_r = jnp.eye(r, dtype=F32)
    kc, ks = jnp.kron(c2 * sc2, eye_r), jnp.kron(sn2 * sc2, eye_r)
    i1 = jnp.arange(s1, dtype=jnp.int32)
    ca, sa = _cos_sin(i1[:, None] * i1[None, :], s1)
    cb, sb = _cos_sin(i2[:, None] * i1[None, :], seq_len)
    sc1 = 1.0 / math.sqrt(s1)
    mc = (ca[None] * cb[:, None, :] - sa[None] * sb[:, None, :]) * sc1
    ms = (sa[None] * cb[:, None, :] + ca[None] * sb[:, None, :]) * sc1
    return kc.astype(BF16), ks.astype(BF16), mc.astype(BF16), ms.astype(BF16)


def _prep_weights(norm_mix_g, w_in, q_norm_g, w_q_up, kv_norm_g, w_kv_up, w_out, norm_ffn_g,
                  w_gate, w_up, w_down, final_norm_g):
    q_rank = q_norm_g.shape[-1]
    kv_rank = kv_norm_g.shape[-1]
    o3 = F_WIDTH + q_rank + kv_rank
    w1 = jnp.concatenate([w_in, _swap_halves(w_in[:, o3:])], axis=1).astype(BF16)
    wq = w_q_up.reshape(q_rank, N_HEADS, QK_HEAD_DIM)
    wq_rope = wq[..., QK_NOPE_DIM:]
    wq = jnp.concatenate([wq, _swap_halves(wq_rope)], axis=-1).reshape(q_rank, N_HEADS * HEAD_PAD)
    wkv = w_kv_up.reshape(kv_rank, N_HEADS, QK_NOPE_DIM + V_HEAD_DIM)
    wkv = jnp.concatenate([wkv[..., :QK_NOPE_DIM].reshape(kv_rank, -1),
                           wkv[..., QK_NOPE_DIM:].reshape(kv_rank, -1)], axis=1)
    return dict(
        g1=norm_mix_g[None, :], w1=w1, d64=_channel_dft(), qg=q_norm_g[None, :], wq=wq.astype(BF16),
        kvg=kv_norm_g[None, :], wkv=wkv.astype(BF16),
        wof=w_out[:F_WIDTH].astype(BF16), woa=w_out[F_WIDTH:].astype(BF16), g2=norm_ffn_g[None, :],
        wg=w_gate.astype(BF16), wu=w_up.astype(BF16), wd=w_down.astype(BF16), gf=final_norm_g[None, :])


def _trunk(x, w):
    b, s, d = x.shape
    tm = 512
    kc, ks, mc, ms = _seq_dft_tables(s, tm // (s // FFT_S1))
    v1, q, k, v = _proj(x, w["g1"], w["w1"], w["d64"], kc, ks, w["qg"], w["wq"], w["kvg"], w["wkv"],
                        _rope_table(s), tm=tm)
    yf = _fft2(mc, ms, v1, kb=8)
    attn_tiles = dict(hb=1, tq=512, tk=2048) if s > 2048 else dict(hb=3, tq=512, tk=s)
    ya = _attention(q, k, v, **attn_tiles)
    return _ffn(x, yf.reshape(b, s, F_WIDTH), ya, w["wof"], w["woa"], w["g2"], w["wg"], w["wu"], w["wd"],
                w["gf"], tm=tm, tf=MXU_DIM)


def kernel(x_prompt, x_sample, norm_mix_g, w_in, q_norm_g, w_q_up, kv_norm_g, w_kv_up, w_out, norm_ffn_g,
           w_gate, w_up, w_down, final_norm_g):
    assert w_in.shape[0] == 1, "single-layer trunk only"
    w = _prep_weights(norm_mix_g[0], w_in[0], q_norm_g[0], w_q_up[0], kv_norm_g[0], w_kv_up[0], w_out[0],
                      norm_ffn_g[0], w_gate[0], w_up[0], w_down[0], final_norm_g)
    return (_trunk(x_prompt, w), _trunk(x_sample, w))
```

```python
import functools
import math

import jax
import jax.numpy as jnp
from jax import lax
from jax.experimental import pallas as pl
from jax.experimental.pallas import tpu as pltpu

LANES = 128
SUBLANES = 8
MXU_DIM = 256
VMEM_BYTES = 64 * 1024 * 1024

N_FGROUPS = 4
FGROUP_DIM = 64
F_WIDTH = N_FGROUPS * FGROUP_DIM
N_HEADS = 6
QK_NOPE_DIM = 128
QK_ROPE_DIM = 64
V_HEAD_DIM = 128
QK_HEAD_DIM = QK_NOPE_DIM + QK_ROPE_DIM
ATTN_WIDTH = N_HEADS * V_HEAD_DIM
ROPE_THETA = 10000.0
EPS = 1e-6
SM_SCALE = 1.0 / math.sqrt(QK_HEAD_DIM)
LOG2E = 1.4426950408889634

HEAD_PAD = 2 * LANES
BF16_SUBLANES = 16
V_ROWS = V_HEAD_DIM + BF16_SUBLANES
FFT_S1 = MXU_DIM

BF16 = jnp.bfloat16
F32 = jnp.float32


def _vmem_limit(nbytes):
    return int(min(nbytes, VMEM_BYTES - 4 * 1024 * 1024))


def _rms(x, g):
    return x * lax.rsqrt(jnp.mean(x * x, axis=-1, keepdims=True) + EPS) * g


def _dot(a, b):
    return jnp.dot(a, b, preferred_element_type=F32)


def _proj_kernel(x_ref, g_ref, w1_ref, d64_ref, kc_ref, ks_ref, qg_ref, wq_ref, kvg_ref, wk_ref, wvt_ref, cs_ref,
                 v1_ref, q_ref, k_ref, vt_ref):
    s2, r, d = x_ref.shape[1:]
    tm = s2 * r
    o1 = F_WIDTH
    o2 = o1 + qg_ref.shape[1]
    o3 = o2 + kvg_ref.shape[1]
    xn = _rms(x_ref[0].reshape(tm, d), g_ref[...]).astype(BF16)
    h = _dot(xn, w1_ref[...])

    z = _dot(h[:, :o1].astype(BF16), d64_ref[...]).astype(BF16)
    v1 = (_dot(kc_ref[...], z[:, :2 * o1]) + _dot(ks_ref[...], z[:, 2 * o1:])).astype(BF16)
    v1_ref[0, 0] = v1[:, :o1].reshape(s2, r, o1)
    v1_ref[0, 1] = v1[:, o1:].reshape(s2, r, o1)

    cs = cs_ref[...].reshape(tm, LANES)
    cqn = (_rms(h[:, o1:o2], qg_ref[...]) * (SM_SCALE * LOG2E)).astype(BF16)
    qf = _dot(cqn, wq_ref[...])
    for hh in range(N_HEADS):
        base = hh * HEAD_PAD
        q_ref[0, hh, :, :, :LANES] = qf[:, base:base + LANES].astype(BF16).reshape(s2, r, LANES)
        q_ref[0, hh, :, :, LANES:] = (qf[:, base + LANES:base + HEAD_PAD] * cs).astype(BF16).reshape(s2, r, LANES)

    ckv = _rms(h[:, o2:o3], kvg_ref[...])
    kn = _dot(ckv.astype(BF16), wk_ref[...])
    vt = _dot(wvt_ref[...], ckv.T.astype(BF16))
    pk = h[:, o3:] * cs
    kr = (pk + pltpu.roll(pk, QK_ROPE_DIM, axis=1)).astype(BF16)
    ones = jnp.ones((V_ROWS - V_HEAD_DIM, tm), BF16)
    for hh in range(N_HEADS):
        k_ref[0, hh, :, :LANES] = kn[:, hh * LANES:(hh + 1) * LANES].astype(BF16)
        k_ref[0, hh, :, LANES:] = kr
        vt_ref[0, hh, 0, :V_HEAD_DIM, :] = vt[hh * V_HEAD_DIM:(hh + 1) * V_HEAD_DIM, :].astype(BF16)
        vt_ref[0, hh, 0, V_HEAD_DIM:, :] = ones


def _proj(x, g, w1, d64, kc, ks, qg, wq, kvg, wk, wvt, cs, *, tm, tk):
    b, s, d = x.shape
    s1 = FFT_S1
    s2 = s // s1
    r = tm // s2
    cpk = tk // tm
    const = lambda shape: pl.BlockSpec(shape, lambda bi, ri: (0,) * len(shape))
    v1, q, k, vt = pl.pallas_call(
        _proj_kernel,
        grid=(b, s1 // r),
        in_specs=[
            pl.BlockSpec((1, s2, r, d), lambda bi, ri: (bi, 0, ri, 0)),
            const(g.shape), const(w1.shape), const(d64.shape), const(kc.shape), const(ks.shape),
            const(qg.shape), const(wq.shape), const(kvg.shape), const(wk.shape), const(wvt.shape),
            pl.BlockSpec((s2, r, LANES), lambda bi, ri: (0, ri, 0)),
        ],
        out_specs=[
            pl.BlockSpec((1, 2, s2, r, F_WIDTH), lambda bi, ri: (bi, 0, 0, ri, 0)),
            pl.BlockSpec((1, N_HEADS, s2, r, HEAD_PAD), lambda bi, ri: (bi, 0, 0, ri, 0)),
            pl.BlockSpec((1, N_HEADS, tm, HEAD_PAD), lambda bi, ri: (bi, 0, ri, 0)),
            pl.BlockSpec((1, N_HEADS, 1, V_ROWS, tm), lambda bi, ri: (bi, 0, ri // cpk, 0, ri % cpk)),
        ],
        out_shape=[jax.ShapeDtypeStruct((b, 2, s2, s1, F_WIDTH), BF16),
                   jax.ShapeDtypeStruct((b, N_HEADS, s2, s1, HEAD_PAD), BF16),
                   jax.ShapeDtypeStruct((b, N_HEADS, s, HEAD_PAD), BF16),
                   jax.ShapeDtypeStruct((b, N_HEADS, s // tk, V_ROWS, tk), BF16)],
        compiler_params=pltpu.CompilerParams(
            dimension_semantics=("arbitrary", "arbitrary"), vmem_limit_bytes=_vmem_limit(48 << 20)),
        name="proj",
    )(x.reshape(b, s2, s1, d), g, w1, d64, kc, ks, qg, wq, kvg, wk, wvt, cs.reshape(s2, s1, LANES))
    return v1, q.reshape(b, N_HEADS, s, HEAD_PAD), k, vt


def _fft2_kernel(mc_ref, ms_ref, vr_ref, vi_ref, y_ref):
    for j in range(mc_ref.shape[0]):
        y = _dot(mc_ref[j], vr_ref[0, 0, j]) + _dot(ms_ref[j], vi_ref[0, 0, j])
        y_ref[0, :, j * F_WIDTH:(j + 1) * F_WIDTH] = y.astype(BF16)


def _fft2(mc, ms, v, *, kb):
    b, _, s2, s1, f = v.shape
    m_spec = pl.BlockSpec((kb, s1, s1), lambda ki, bi: (ki, 0, 0))
    return pl.pallas_call(
        _fft2_kernel,
        grid=(s2 // kb, b),
        in_specs=[m_spec, m_spec,
                  pl.BlockSpec((1, 1, kb, s1, f), lambda ki, bi: (bi, 0, ki, 0, 0)),
                  pl.BlockSpec((1, 1, kb, s1, f), lambda ki, bi: (bi, 1, ki, 0, 0))],
        out_specs=pl.BlockSpec((1, s1, kb * f), lambda ki, bi: (bi, 0, ki)),
        out_shape=jax.ShapeDtypeStruct((b, s1, s2 * f), BF16),
        compiler_params=pltpu.CompilerParams(
            dimension_semantics=("arbitrary", "arbitrary"), vmem_limit_bytes=_vmem_limit(32 << 20)),
        name="fft2",
    )(mc, ms, v, v)


ATTN_UNROLL = 8
ATTN_AHEAD = 2
ATTN_SLOTS = 4


def _attn_kernel(q_ref, k_ref, vt_ref, o_ref, s_s, mx_s, m_s, acc_s, *, tq, tk):
    hb = q_ref.shape[1]
    nq = q_ref.shape[2] // tq
    nk = k_ref.shape[2] // tk
    nsteps = hb * nq * nk
    assert nsteps % ATTN_UNROLL == 0 and ATTN_UNROLL % nk == 0 and ATTN_UNROLL % ATTN_SLOTS == 0

    def tile(t0, i):
        r = lax.div(t0, nk) + i // nk
        return lax.div(r, nq), pl.multiple_of(lax.rem(r, nq) * tq, tq), i % nk

    def scores(t0, i):
        hh, qs, kj = tile(t0, i)
        s = lax.dot_general(k_ref[0, hh, kj * tk:(kj + 1) * tk, :], q_ref[0, hh, pl.ds(qs, tq), :],
                            (((1,), (1,)), ((), ())), preferred_element_type=F32)
        s_s[i % ATTN_SLOTS] = s
        mx_s[i % ATTN_SLOTS] = jnp.max(s.reshape(tk // SUBLANES, SUBLANES, tq), axis=0)

    def softmax_pv(t0, i):
        hh, qs, kj = tile(t0, i)
        slot = i % ATTN_SLOTS
        mx = jnp.max(mx_s[slot], axis=0, keepdims=True)
        if kj == 0:
            m_new = mx
        else:
            m_prev = m_s[...]
            m_new = jnp.maximum(m_prev, mx)
        p = jnp.exp2(s_s[slot] - m_new).astype(BF16)
        acc = _dot(vt_ref[0, hh, kj], p)
        if kj > 0:
            acc = jnp.exp2(m_prev - m_new) * acc_s[...] + acc
        if kj == nk - 1:
            o = acc[:V_HEAD_DIM] / acc[V_HEAD_DIM:V_HEAD_DIM + 1]
            o_ref[0, hh, pl.ds(qs, tq), :] = o.T.astype(BF16)
        else:
            acc_s[...] = acc
            m_s[...] = m_new

    def steps(t0, last):
        for i in range(ATTN_UNROLL):
            if not (last and i + ATTN_AHEAD >= ATTN_UNROLL):
                scores(t0, i + ATTN_AHEAD)
            softmax_pv(t0, i)

    for i in range(ATTN_AHEAD):
        scores(jnp.int32(0), i)

    def body(u, carry):
        steps(u * ATTN_UNROLL, False)
        return carry

    lax.fori_loop(0, nsteps // ATTN_UNROLL - 1, body, 0)
    steps(jnp.int32(nsteps - ATTN_UNROLL), True)


def _attention(q, k, vt, *, hb, tq):
    b, h, s, w = q.shape
    nk, v_rows, tk = vt.shape[2:]
    spec = pl.BlockSpec((1, hb, s, w), lambda bi, hi: (bi, hi, 0, 0))
    return pl.pallas_call(
        functools.partial(_attn_kernel, tq=tq, tk=tk),
        grid=(b, h // hb),
        in_specs=[spec, spec, pl.BlockSpec((1, hb, nk, v_rows, tk), lambda bi, hi: (bi, hi, 0, 0, 0))],
        out_specs=pl.BlockSpec((1, hb, s, V_HEAD_DIM), lambda bi, hi: (bi, hi, 0, 0)),
        out_shape=jax.ShapeDtypeStruct((b, h, s, V_HEAD_DIM), BF16),
        scratch_shapes=[pltpu.VMEM((ATTN_SLOTS, tk, tq), F32), pltpu.VMEM((ATTN_SLOTS, SUBLANES, tq), F32),
                        pltpu.VMEM((1, tq), F32), pltpu.VMEM((v_rows, tq), F32)],
        compiler_params=pltpu.CompilerParams(
            dimension_semantics=("arbitrary", "arbitrary"), vmem_limit_bytes=_vmem_limit(56 << 20)),
        name="attention",
    )(q, k, vt)


def _ffn_kernel(x_ref, yf_ref, ya_ref, wof_ref, woa_ref, g2_ref, wg_ref, wu_ref, wd_ref, gf_ref, o_ref, *, tf):
    ya = jnp.concatenate([ya_ref[0, hh] for hh in range(ya_ref.shape[1])], axis=1)
    x1 = x_ref[0] + _dot(yf_ref[0], wof_ref[...]) + _dot(ya, woa_ref[...])
    xn = _rms(x1, g2_ref[...]).astype(BF16)
    ffn = None
    for c in range(0, wg_ref.shape[1], tf):
        gate = _dot(xn, wg_ref[:, c:c + tf])
        up = _dot(xn, wu_ref[:, c:c + tf])
        hmid = (gate * jax.nn.sigmoid(gate) * up).astype(BF16)
        part = _dot(hmid, wd_ref[c:c + tf, :])
        ffn = part if ffn is None else ffn + part
    o_ref[0] = _rms(x1 + ffn, gf_ref[...])


def _ffn(x, yf, ya, wof, woa, g2, wg, wu, wd, gf, *, tm, tf):
    b, s, d = x.shape
    h = ya.shape[1]
    const = lambda shape: pl.BlockSpec(shape, lambda bi, si: (0,) * len(shape), pipeline_mode=pl.Buffered(1))
    row = lambda width: pl.BlockSpec((1, tm, width), lambda bi, si: (bi, si, 0))
    return pl.pallas_call(
        functools.partial(_ffn_kernel, tf=tf),
        grid=(b, s // tm),
        in_specs=[row(d), row(yf.shape[2]),
                  pl.BlockSpec((1, h, tm, V_HEAD_DIM), lambda bi, si: (bi, 0, si, 0)),
                  const(wof.shape), const(woa.shape), const(g2.shape), const(wg.shape), const(wu.shape),
                  const(wd.shape), const(gf.shape)],
        out_specs=row(d),
        out_shape=jax.ShapeDtypeStruct((b, s, d), F32),
        compiler_params=pltpu.CompilerParams(
            dimension_semantics=("arbitrary", "arbitrary"), vmem_limit_bytes=_vmem_limit(56 << 20)),
        name="ffn",
    )(x, yf, ya, wof, woa, g2, wg, wu, wd, gf)


def _swap_halves(w):
    half = w.shape[-1] // 2
    return jnp.concatenate([-w[..., half:], w[..., :half]], axis=-1)


def _rope_table(seq_len):
    inv_freq = 1.0 / (ROPE_THETA ** (jnp.arange(0, QK_ROPE_DIM, 2, dtype=F32) / QK_ROPE_DIM))
    ang = jnp.arange(seq_len, dtype=F32)[:, None] * inv_freq[None, :]
    cos, sin = jnp.cos(ang), jnp.sin(ang)
    return jnp.concatenate([cos, cos, sin, sin], axis=-1)


def _cos_sin(num, den):
    ang = (2.0 * math.pi / den) * (num % den).astype(F32)
    return jnp.cos(ang), jnp.sin(ang)


def _channel_dft():
    j = jnp.arange(FGROUP_DIM, dtype=jnp.int32)
    c, s = _cos_sin(j[:, None] * j[None, :], FGROUP_DIM)
    eye = jnp.eye(N_FGROUPS, dtype=F32)
    scale = 1.0 / math.sqrt(FGROUP_DIM)
    re, im = jnp.kron(eye, c * scale), jnp.kron(eye, -s * scale)
    return jnp.concatenate([re, im, im, -re], axis=1).astype(BF16)


def _seq_dft_tables(seq_len, r):
    s1 = FFT_S1
    s2 = seq_len // s1
    i2 = jnp.arange(s2, dtype=jnp.int32)
    c2, sn2 = _cos_sin(i2[:, None] * i2[None, :], s2)
    sc2 = 1.0 / math.sqrt(s2)
    eye_r = jnp.eye(r, dtype=F32)
    kc, ks = jnp.kron(c2 * sc2, eye_r), jnp.kron(sn2 * sc2, eye_r)
    i1 = jnp.arange(s1, dtype=jnp.int32)
    ca, sa = _cos_sin(i1[:, None] * i1[None, :], s1)
    cb, sb = _cos_sin(i2[:, None] * i1[None, :], seq_len)
    sc1 = 1.0 / math.sqrt(s1)
    mc = (ca[None] * cb[:, None, :] - sa[None] * sb[:, None, :]) * sc1
    ms = (sa[None] * cb[:, None, :] + ca[None] * sb[:, None, :]) * sc1
    return kc.astype(BF16), ks.astype(BF16), mc.astype(BF16), ms.astype(BF16)


def _prep_weights(norm_mix_g, w_in, q_norm_g, w_q_up, kv_norm_g, w_kv_up, w_out, norm_ffn_g,
                  w_gate, w_up, w_down, final_norm_g):
    q_rank = q_norm_g.shape[-1]
    kv_rank = kv_norm_g.shape[-1]
    o3 = F_WIDTH + q_rank + kv_rank
    w1 = jnp.concatenate([w_in, _swap_halves(w_in[:, o3:])], axis=1).astype(BF16)
    wq = w_q_up.reshape(q_rank, N_HEADS, QK_HEAD_DIM)
    wq_rope = wq[..., QK_NOPE_DIM:]
    wq = jnp.concatenate([wq, _swap_halves(wq_rope)], axis=-1).reshape(q_rank, N_HEADS * HEAD_PAD)
    wkv = w_kv_up.reshape(kv_rank, N_HEADS, QK_NOPE_DIM + V_HEAD_DIM)
    wk = wkv[..., :QK_NOPE_DIM].reshape(kv_rank, -1)
    wvt = wkv[..., QK_NOPE_DIM:].reshape(kv_rank, -1).T
    return dict(
        g1=norm_mix_g[None, :], w1=w1, d64=_channel_dft(), qg=q_norm_g[None, :], wq=wq.astype(BF16),
        kvg=kv_norm_g[None, :], wk=wk.astype(BF16), wvt=wvt.astype(BF16),
        wof=w_out[:F_WIDTH].astype(BF16), woa=w_out[F_WIDTH:].astype(BF16), g2=norm_ffn_g[None, :],
        wg=w_gate.astype(BF16), wu=w_up.astype(BF16), wd=w_down.astype(BF16), gf=final_norm_g[None, :])


def _trunk(x, w):
    b, s, d = x.shape
    tm = 512
    kc, ks, mc, ms = _seq_dft_tables(s, tm // (s // FFT_S1))
    tk, hb = min(s, 2048), (1 if s > 2048 else N_HEADS)
    v1, q, k, vt = _proj(x, w["g1"], w["w1"], w["d64"], kc, ks, w["qg"], w["wq"], w["kvg"], w["wk"], w["wvt"],
                         _rope_table(s), tm=tm, tk=tk)
    yf = _fft2(mc, ms, v1, kb=8)
    ya = _attention(q, k, vt, hb=hb, tq=256)
    return _ffn(x, yf.reshape(b, s, F_WIDTH), ya, w["wof"], w["woa"], w["g2"], w["wg"], w["wu"], w["wd"],
                w["gf"], tm=tm, tf=MXU_DIM)


def kernel(x_prompt, x_sample, norm_mix_g, w_in, q_norm_g, w_q_up, kv_norm_g, w_kv_up, w_out, norm_ffn_g,
           w_gate, w_up, w_down, final_norm_g):
    assert w_in.shape[0] == 1, "single-layer trunk only"
    w = _prep_weights(norm_mix_g[0], w_in[0], q_norm_g[0], w_q_up[0], kv_norm_g[0], w_kv_up[0], w_out[0],
                      norm_ffn_g[0], w_gate[0], w_up[0], w_down[0], final_norm_g)
    return (_trunk(x_prompt, w), _trunk(x_sample, w))
```

```python
import functools
import math

import jax
import jax.numpy as jnp
from jax import lax
from jax.experimental import pallas as pl
from jax.experimental.pallas import tpu as pltpu

LANES = 128
SUBLANES = 8
MXU_DIM = 256
VMEM_BYTES = 64 * 1024 * 1024

N_FGROUPS = 4
FGROUP_DIM = 64
F_WIDTH = N_FGROUPS * FGROUP_DIM
N_HEADS = 6
QK_NOPE_DIM = 128
QK_ROPE_DIM = 64
V_HEAD_DIM = 128
QK_HEAD_DIM = QK_NOPE_DIM + QK_ROPE_DIM
ATTN_WIDTH = N_HEADS * V_HEAD_DIM
ROPE_THETA = 10000.0
EPS = 1e-6
SM_SCALE = 1.0 / math.sqrt(QK_HEAD_DIM)
LOG2E = 1.4426950408889634

HEAD_PAD = 2 * LANES
BF16_SUBLANES = 16
V_ROWS = V_HEAD_DIM + BF16_SUBLANES
FFT_S1 = MXU_DIM

BF16 = jnp.bfloat16
F32 = jnp.float32


def _vmem_limit(nbytes):
    return int(min(nbytes, VMEM_BYTES - 4 * 1024 * 1024))


def _rms(x, g):
    return x * lax.rsqrt(jnp.mean(x * x, axis=-1, keepdims=True) + EPS) * g


def _dot(a, b):
    return jnp.dot(a, b, preferred_element_type=F32)


def _proj_kernel(x_ref, g_ref, w1_ref, d64_ref, kc_ref, ks_ref, qg_ref, wq_ref, kvg_ref, wk_ref, wvt_ref, cs_ref,
                 v1_ref, q_ref, k_ref, vt_ref):
    s2, r, d = x_ref.shape[1:]
    tm = s2 * r
    o1 = F_WIDTH
    o2 = o1 + qg_ref.shape[1]
    o3 = o2 + kvg_ref.shape[1]
    xn = _rms(x_ref[0].reshape(tm, d), g_ref[...]).astype(BF16)
    h = _dot(xn, w1_ref[...])

    z = _dot(h[:, :o1].astype(BF16), d64_ref[...])
    za = z.reshape(s2, r, 2 * o1)
    zb = jnp.concatenate([z[:, o1:], -z[:, :o1]], axis=1).reshape(s2, r, 2 * o1)
    rs = kc_ref.shape[0] // s2
    parts = []
    for j in range(0, r, rs):
        sub = lambda a: a[:, j:j + rs, :].reshape(s2 * rs, 2 * o1).astype(BF16)
        parts.append((_dot(kc_ref[...], sub(za)) + _dot(ks_ref[...], sub(zb))).reshape(s2, rs, 2 * o1))
    v1 = jnp.concatenate(parts, axis=1)
    v1_ref[0, 0] = v1[:, :, :o1].astype(BF16)
    v1_ref[0, 1] = v1[:, :, o1:].astype(BF16)

    cs = cs_ref[...].reshape(tm, LANES)
    cqn = (_rms(h[:, o1:o2], qg_ref[...]) * (SM_SCALE * LOG2E)).astype(BF16)
    qf = _dot(cqn, wq_ref[...])
    for hh in range(N_HEADS):
        base = hh * HEAD_PAD
        q_ref[0, hh, :, :, :LANES] = qf[:, base:base + LANES].astype(BF16).reshape(s2, r, LANES)
        q_ref[0, hh, :, :, LANES:] = (qf[:, base + LANES:base + HEAD_PAD] * cs).astype(BF16).reshape(s2, r, LANES)

    ckv = _rms(h[:, o2:o3], kvg_ref[...])
    kn = _dot(ckv.astype(BF16), wk_ref[...])
    vt = _dot(wvt_ref[...], ckv.T.astype(BF16))
    pk = h[:, o3:] * cs
    kr = (pk + pltpu.roll(pk, QK_ROPE_DIM, axis=1)).astype(BF16)
    ones = jnp.ones((V_ROWS - V_HEAD_DIM, tm), BF16)
    for hh in range(N_HEADS):
        k_ref[0, hh, :, :LANES] = kn[:, hh * LANES:(hh + 1) * LANES].astype(BF16)
        k_ref[0, hh, :, LANES:] = kr
        vt_ref[0, hh, 0, :V_HEAD_DIM, :] = vt[hh * V_HEAD_DIM:(hh + 1) * V_HEAD_DIM, :].astype(BF16)
        vt_ref[0, hh, 0, V_HEAD_DIM:, :] = ones


def _proj(x, g, w1, d64, kc, ks, qg, wq, kvg, wk, wvt, cs, *, tm, tk):
    b, s, d = x.shape
    s1 = FFT_S1
    s2 = s // s1
    r = tm // s2
    cpk = tk // tm
    const = lambda shape: pl.BlockSpec(shape, lambda bi, ri: (0,) * len(shape))
    v1, q, k, vt = pl.pallas_call(
        _proj_kernel,
        grid=(b, s1 // r),
        in_specs=[
            pl.BlockSpec((1, s2, r, d), lambda bi, ri: (bi, 0, ri, 0)),
            const(g.shape), const(w1.shape), const(d64.shape), const(kc.shape), const(ks.shape),
            const(qg.shape), const(wq.shape), const(kvg.shape), const(wk.shape), const(wvt.shape),
            pl.BlockSpec((s2, r, LANES), lambda bi, ri: (0, ri, 0)),
        ],
        out_specs=[
            pl.BlockSpec((1, 2, s2, r, F_WIDTH), lambda bi, ri: (bi, 0, 0, ri, 0)),
            pl.BlockSpec((1, N_HEADS, s2, r, HEAD_PAD), lambda bi, ri: (bi, 0, 0, ri, 0)),
            pl.BlockSpec((1, N_HEADS, tm, HEAD_PAD), lambda bi, ri: (bi, 0, ri, 0)),
            pl.BlockSpec((1, N_HEADS, 1, V_ROWS, tm), lambda bi, ri: (bi, 0, ri // cpk, 0, ri % cpk)),
        ],
        out_shape=[jax.ShapeDtypeStruct((b, 2, s2, s1, F_WIDTH), BF16),
                   jax.ShapeDtypeStruct((b, N_HEADS, s2, s1, HEAD_PAD), BF16),
                   jax.ShapeDtypeStruct((b, N_HEADS, s, HEAD_PAD), BF16),
                   jax.ShapeDtypeStruct((b, N_HEADS, s // tk, V_ROWS, tk), BF16)],
        compiler_params=pltpu.CompilerParams(
            dimension_semantics=("arbitrary", "arbitrary"), vmem_limit_bytes=_vmem_limit(48 << 20)),
        name="proj",
    )(x.reshape(b, s2, s1, d), g, w1, d64, kc, ks, qg, wq, kvg, wk, wvt, cs.reshape(s2, s1, LANES))
    return v1, q.reshape(b, N_HEADS, s, HEAD_PAD), k, vt


def _fft2_kernel(mc_ref, ms_ref, vr_ref, vi_ref, y_ref):
    for j in range(mc_ref.shape[0]):
        y = _dot(mc_ref[j], vr_ref[0, 0, j]) + _dot(ms_ref[j], vi_ref[0, 0, j])
        y_ref[0, :, j * F_WIDTH:(j + 1) * F_WIDTH] = y.astype(BF16)


def _fft2(mc, ms, v, *, kb):
    b, _, s2, s1, f = v.shape
    m_spec = pl.BlockSpec((kb, s1, s1), lambda ki, bi: (ki, 0, 0))
    return pl.pallas_call(
        _fft2_kernel,
        grid=(s2 // kb, b),
        in_specs=[m_spec, m_spec,
                  pl.BlockSpec((1, 1, kb, s1, f), lambda ki, bi: (bi, 0, ki, 0, 0)),
                  pl.BlockSpec((1, 1, kb, s1, f), lambda ki, bi: (bi, 1, ki, 0, 0))],
        out_specs=pl.BlockSpec((1, s1, kb * f), lambda ki, bi: (bi, 0, ki)),
        out_shape=jax.ShapeDtypeStruct((b, s1, s2 * f), BF16),
        compiler_params=pltpu.CompilerParams(
            dimension_semantics=("arbitrary", "arbitrary"), vmem_limit_bytes=_vmem_limit(32 << 20)),
        name="fft2",
    )(mc, ms, v, v)


ATTN_UNROLL = 16
ATTN_AHEAD = 2
ATTN_SLOTS = 4


def _attn_kernel(q_ref, k_ref, vt_ref, o_ref, s_s, mx_s, acc_s, m_s, *, tq, tk):
    hb = q_ref.shape[1]
    nq = q_ref.shape[2] // tq
    nk = k_ref.shape[2] // tk
    nsteps = hb * nq * nk
    assert nsteps % ATTN_UNROLL == 0 and ATTN_UNROLL % nk == 0 and ATTN_UNROLL % ATTN_SLOTS == 0

    def tile(t0, i):
        r = lax.div(t0, nk) + i // nk
        return lax.div(r, nq), pl.multiple_of(lax.rem(r, nq) * tq, tq), i % nk

    def scores(t0, i):
        hh, qs, kj = tile(t0, i)
        s = lax.dot_general(k_ref[0, hh, kj * tk:(kj + 1) * tk, :], q_ref[0, hh, pl.ds(qs, tq), :],
                            (((1,), (1,)), ((), ())), preferred_element_type=F32)
        s_s[i % ATTN_SLOTS] = s
        mx_s[i % ATTN_SLOTS] = jnp.max(s.reshape(tk // SUBLANES, SUBLANES, tq), axis=0)

    def softmax_pv(t0, i):
        hh, qs, kj = tile(t0, i)
        slot = i % ATTN_SLOTS
        mx = jnp.max(mx_s[slot], axis=0, keepdims=True)
        if kj == 0:
            m_new = mx
        else:
            m_prev = m_s[...]
            m_new = jnp.maximum(m_prev, mx)
        p = jnp.exp2(s_s[slot] - m_new).astype(BF16)
        acc = _dot(vt_ref[0, hh, kj], p)
        if kj > 0:
            acc = jnp.exp2(m_prev - m_new) * acc_s[...] + acc
        if kj == nk - 1:
            o = acc[:V_HEAD_DIM] / acc[V_HEAD_DIM:V_HEAD_DIM + 1]
            o_ref[0, hh, pl.ds(qs, tq), :] = o.T.astype(BF16)
        else:
            acc_s[...] = acc
            m_s[...] = m_new

    def steps(t0, last):
        for i in range(ATTN_UNROLL):
            if not (last and i + ATTN_AHEAD >= ATTN_UNROLL):
                scores(t0, i + ATTN_AHEAD)
            softmax_pv(t0, i)

    for i in range(ATTN_AHEAD):
        scores(jnp.int32(0), i)

    def body(u, carry):
        steps(u * ATTN_UNROLL, False)
        return carry

    lax.fori_loop(0, nsteps // ATTN_UNROLL - 1, body, 0)
    steps(jnp.int32(nsteps - ATTN_UNROLL), True)


def _attention(q, k, vt, *, hb, tq):
    b, h, s, w = q.shape
    nk, v_rows, tk = vt.shape[2:]
    spec = pl.BlockSpec((1, hb, s, w), lambda bi, hi: (bi, hi, 0, 0))
    return pl.pallas_call(
        functools.partial(_attn_kernel, tq=tq, tk=tk),
        grid=(b, h // hb),
        in_specs=[spec, spec, pl.BlockSpec((1, hb, nk, v_rows, tk), lambda bi, hi: (bi, hi, 0, 0, 0))],
        out_specs=pl.BlockSpec((1, hb, s, V_HEAD_DIM), lambda bi, hi: (bi, hi, 0, 0)),
        out_shape=jax.ShapeDtypeStruct((b, h, s, V_HEAD_DIM), BF16),
        scratch_shapes=[pltpu.VMEM((ATTN_SLOTS, tk, tq), F32), pltpu.VMEM((ATTN_SLOTS, SUBLANES, tq), F32),
                        pltpu.VMEM((v_rows, tq), F32), pltpu.VMEM((1, tq), F32)],
        compiler_params=pltpu.CompilerParams(
            dimension_semantics=("arbitrary", "arbitrary"), vmem_limit_bytes=_vmem_limit(56 << 20)),
        name="attention",
    )(q, k, vt)


def _ffn_kernel(x_ref, yf_ref, ya_ref, wof_ref, woa_ref, g2_ref, wg_ref, wu_ref, wd_ref, gf_ref, o_ref, *, tf):
    ya = jnp.concatenate([ya_ref[0, hh] for hh in range(ya_ref.shape[1])], axis=1)
    x1 = x_ref[0] + _dot(yf_ref[0], wof_ref[...]) + _dot(ya, woa_ref[...])
    xn = _rms(x1, g2_ref[...]).astype(BF16)
    ffn = None
    dff = wg_ref.shape[1]
    for c in range(0, dff, tf):
        e = min(c + tf, dff)
        gate = _dot(xn, wg_ref[:, c:e])
        up = _dot(xn, wu_ref[:, c:e])
        hmid = (gate * jax.nn.sigmoid(gate) * up).astype(BF16)
        part = _dot(hmid, wd_ref[c:e, :])
        ffn = part if ffn is None else ffn + part
    o_ref[0] = _rms(x1 + ffn, gf_ref[...])


def _ffn(x, yf, ya, wof, woa, g2, wg, wu, wd, gf, *, tm, tf):
    b, s, d = x.shape
    h = ya.shape[1]
    const = lambda shape: pl.BlockSpec(shape, lambda bi, si: (0,) * len(shape), pipeline_mode=pl.Buffered(1))
    row = lambda width: pl.BlockSpec((1, tm, width), lambda bi, si: (bi, si, 0))
    return pl.pallas_call(
        functools.partial(_ffn_kernel, tf=tf),
        grid=(b, s // tm),
        in_specs=[row(d), row(yf.shape[2]),
                  pl.BlockSpec((1, h, tm, V_HEAD_DIM), lambda bi, si: (bi, 0, si, 0)),
                  const(wof.shape), const(woa.shape), const(g2.shape), const(wg.shape), const(wu.shape),
                  const(wd.shape), const(gf.shape)],
        out_specs=row(d),
        out_shape=jax.ShapeDtypeStruct((b, s, d), F32),
        compiler_params=pltpu.CompilerParams(
            dimension_semantics=("arbitrary", "arbitrary"), vmem_limit_bytes=_vmem_limit(56 << 20)),
        name="ffn",
    )(x, yf, ya, wof, woa, g2, wg, wu, wd, gf)


def _swap_halves(w):
    half = w.shape[-1] // 2
    return jnp.concatenate([-w[..., half:], w[..., :half]], axis=-1)


def _rope_table(seq_len):
    inv_freq = 1.0 / (ROPE_THETA ** (jnp.arange(0, QK_ROPE_DIM, 2, dtype=F32) / QK_ROPE_DIM))
    ang = jnp.arange(seq_len, dtype=F32)[:, None] * inv_freq[None, :]
    cos, sin = jnp.cos(ang), jnp.sin(ang)
    return jnp.concatenate([cos, cos, sin, sin], axis=-1)


def _cos_sin(num, den):
    ang = (2.0 * math.pi / den) * (num % den).astype(F32)
    return jnp.cos(ang), jnp.sin(ang)


def _channel_dft():
    j = jnp.arange(FGROUP_DIM, dtype=jnp.int32)
    c, s = _cos_sin(j[:, None] * j[None, :], FGROUP_DIM)
    eye = jnp.eye(N_FGROUPS, dtype=F32)
    scale = 1.0 / math.sqrt(FGROUP_DIM)
    re, im = jnp.kron(eye, c * scale), jnp.kron(eye, -s * scale)
    return jnp.concatenate([re, im], axis=1).astype(BF16)


def _seq_dft_tables(seq_len):
    s1 = FFT_S1
    s2 = seq_len // s1
    i2 = jnp.arange(s2, dtype=jnp.int32)
    c2, sn2 = _cos_sin(i2[:, None] * i2[None, :], s2)
    sc2 = 1.0 / math.sqrt(s2)
    eye_r = jnp.eye(MXU_DIM // s2, dtype=F32)
    kc, ks = jnp.kron(c2 * sc2, eye_r), jnp.kron(sn2 * sc2, eye_r)
    i1 = jnp.arange(s1, dtype=jnp.int32)
    ca, sa = _cos_sin(i1[:, None] * i1[None, :], s1)
    cb, sb = _cos_sin(i2[:, None] * i1[None, :], seq_len)
    sc1 = 1.0 / math.sqrt(s1)
    mc = (ca[None] * cb[:, None, :] - sa[None] * sb[:, None, :]) * sc1
    ms = (sa[None] * cb[:, None, :] + ca[None] * sb[:, None, :]) * sc1
    return kc.astype(BF16), ks.astype(BF16), mc.astype(BF16), ms.astype(BF16)


def _prep_weights(norm_mix_g, w_in, q_norm_g, w_q_up, kv_norm_g, w_kv_up, w_out, norm_ffn_g,
                  w_gate, w_up, w_down, final_norm_g):
    q_rank = q_norm_g.shape[-1]
    kv_rank = kv_norm_g.shape[-1]
    o3 = F_WIDTH + q_rank + kv_rank
    w1 = jnp.concatenate([w_in, _swap_halves(w_in[:, o3:])], axis=1).astype(BF16)
    wq = w_q_up.reshape(q_rank, N_HEADS, QK_HEAD_DIM)
    wq_rope = wq[..., QK_NOPE_DIM:]
    wq = jnp.concatenate([wq, _swap_halves(wq_rope)], axis=-1).reshape(q_rank, N_HEADS * HEAD_PAD)
    wkv = w_kv_up.reshape(kv_rank, N_HEADS, QK_NOPE_DIM + V_HEAD_DIM)
    wk = wkv[..., :QK_NOPE_DIM].reshape(kv_rank, -1)
    wvt = wkv[..., QK_NOPE_DIM:].reshape(kv_rank, -1).T
    return dict(
        g1=norm_mix_g[None, :], w1=w1, d64=_channel_dft(), qg=q_norm_g[None, :], wq=wq.astype(BF16),
        kvg=kv_norm_g[None, :], wk=wk.astype(BF16), wvt=wvt.astype(BF16),
        wof=w_out[:F_WIDTH].astype(BF16), woa=w_out[F_WIDTH:].astype(BF16), g2=norm_ffn_g[None, :],
        wg=w_gate.astype(BF16), wu=w_up.astype(BF16), wd=w_down.astype(BF16), gf=final_norm_g[None, :])


def _trunk(x, w):
    b, s, d = x.shape
    tm = 512
    kc, ks, mc, ms = _seq_dft_tables(s)
    tk, hb = min(s, 2048), (1 if s > 2048 else N_HEADS)
    v1, q, k, vt = _proj(x, w["g1"], w["w1"], w["d64"], kc, ks, w["qg"], w["wq"], w["kvg"], w["wk"], w["wvt"],
                         _rope_table(s), tm=2 * tm, tk=tk)
    yf = _fft2(mc, ms, v1, kb=8)
    ya = _attention(q, k, vt, hb=hb, tq=256)
    return _ffn(x, yf.reshape(b, s, F_WIDTH), ya, w["wof"], w["woa"], w["g2"], w["wg"], w["wu"], w["wd"],
                w["gf"], tm=tm, tf=MXU_DIM)


def kernel(x_prompt, x_sample, norm_mix_g, w_in, q_norm_g, w_q_up, kv_norm_g, w_kv_up, w_out, norm_ffn_g,
           w_gate, w_up, w_down, final_norm_g):
    assert w_in.shape[0] == 1, "single-layer trunk only"
    w = _prep_weights(norm_mix_g[0], w_in[0], q_norm_g[0], w_q_up[0], kv_norm_g[0], w_kv_up[0], w_out[0],
                      norm_ffn_g[0], w_gate[0], w_up[0], w_down[0], final_norm_g)
    return (_trunk(x_prompt, w), _trunk(x_sample, w))
```

```python
import functools
import math

import jax
import jax.numpy as jnp
from jax import lax
from jax.experimental import pallas as pl
from jax.experimental.pallas import tpu as pltpu

LANES = 128
SUBLANES = 8
MXU_DIM = 256
VMEM_BYTES = 64 * 1024 * 1024

N_FGROUPS = 4
FGROUP_DIM = 64
F_WIDTH = N_FGROUPS * FGROUP_DIM
N_HEADS = 6
QK_NOPE_DIM = 128
QK_ROPE_DIM = 64
V_HEAD_DIM = 128
QK_HEAD_DIM = QK_NOPE_DIM + QK_ROPE_DIM
ROPE_THETA = 10000.0
EPS = 1e-6
SM_SCALE = 1.0 / math.sqrt(QK_HEAD_DIM)
LOG2E = 1.4426950408889634

HEAD_PAD = 2 * LANES
BF16_SUBLANES = 16
V_ROWS = V_HEAD_DIM + BF16_SUBLANES
FFT_S1 = MXU_DIM

BF16 = jnp.bfloat16
F32 = jnp.float32


MIB = 1024 * 1024
VMEM_LIMITS = dict(proj=48 * MIB, fft2=32 * MIB, attention=56 * MIB, ffn=56 * MIB)
assert max(VMEM_LIMITS.values()) < VMEM_BYTES


def _tiles(seq_len):
    return dict(
        proj_tokens=1024,
        ffn_tokens=1024,
        fft2_k2=8,
        attn_q=MXU_DIM,
        attn_kv=min(seq_len, 2048),
        attn_heads=1 if seq_len > 2048 else N_HEADS,
    )


def _rms(x, g):
    return x * lax.rsqrt(jnp.mean(x * x, axis=-1, keepdims=True) + EPS) * g


def _dot(a, b):
    return jnp.dot(a, b, preferred_element_type=F32)


def _proj_kernel(x_ref, g_ref, w1_ref, d64_ref, kc_ref, ks_ref, qg_ref, wq_ref, kvg_ref, wk_ref, wvt_ref, cs_ref,
                 v1_ref, q_ref, k_ref, vt_ref):
    s2, r, d = x_ref.shape[1:]
    tm = s2 * r
    o1 = F_WIDTH
    o2 = o1 + qg_ref.shape[1]
    o3 = o2 + kvg_ref.shape[1]
    xn = _rms(x_ref[0].reshape(tm, d), g_ref[...]).astype(BF16)
    h = _dot(xn, w1_ref[...])

    z = _dot(h[:, :o1].astype(BF16), d64_ref[...])
    za = z.reshape(s2, r, 2 * o1)
    zb = jnp.concatenate([z[:, o1:], -z[:, :o1]], axis=1).reshape(s2, r, 2 * o1)
    rs = kc_ref.shape[0] // s2
    parts = []
    for j in range(0, r, rs):
        sub = lambda a: a[:, j:j + rs, :].reshape(s2 * rs, 2 * o1).astype(BF16)
        parts.append((_dot(kc_ref[...], sub(za)) + _dot(ks_ref[...], sub(zb))).reshape(s2, rs, 2 * o1))
    v1 = jnp.concatenate(parts, axis=1)
    v1_ref[0, 0] = v1[:, :, :o1].astype(BF16)
    v1_ref[0, 1] = v1[:, :, o1:].astype(BF16)

    cs = cs_ref[...].reshape(tm, LANES)
    cqn = (_rms(h[:, o1:o2], qg_ref[...]) * (SM_SCALE * LOG2E)).astype(BF16)
    qf = _dot(cqn, wq_ref[...])
    for hh in range(N_HEADS):
        base = hh * HEAD_PAD
        q_ref[0, hh, :, :, :LANES] = qf[:, base:base + LANES].astype(BF16).reshape(s2, r, LANES)
        q_ref[0, hh, :, :, LANES:] = (qf[:, base + LANES:base + HEAD_PAD] * cs).astype(BF16).reshape(s2, r, LANES)

    ckv = _rms(h[:, o2:o3], kvg_ref[...])
    kn = _dot(ckv.astype(BF16), wk_ref[...])
    vt = _dot(wvt_ref[...], ckv.T.astype(BF16))
    pk = h[:, o3:] * cs
    kr = (pk + pltpu.roll(pk, QK_ROPE_DIM, axis=1)).astype(BF16)
    ones = jnp.ones((V_ROWS - V_HEAD_DIM, tm), BF16)
    for hh in range(N_HEADS):
        k_ref[0, hh, :, :LANES] = kn[:, hh * LANES:(hh + 1) * LANES].astype(BF16)
        k_ref[0, hh, :, LANES:] = kr
        vt_ref[0, hh, 0, :V_HEAD_DIM, :] = vt[hh * V_HEAD_DIM:(hh + 1) * V_HEAD_DIM, :].astype(BF16)
        vt_ref[0, hh, 0, V_HEAD_DIM:, :] = ones


def _proj(x, g, w1, d64, kc, ks, qg, wq, kvg, wk, wvt, cs, *, tm, tk):
    b, s, d = x.shape
    s1 = FFT_S1
    s2 = s // s1
    r = tm // s2
    cpk = tk // tm
    const = lambda shape: pl.BlockSpec(shape, lambda bi, ri: (0,) * len(shape))
    v1, q, k, vt = pl.pallas_call(
        _proj_kernel,
        grid=(b, s1 // r),
        in_specs=[
            pl.BlockSpec((1, s2, r, d), lambda bi, ri: (bi, 0, ri, 0)),
            const(g.shape), const(w1.shape), const(d64.shape), const(kc.shape), const(ks.shape),
            const(qg.shape), const(wq.shape), const(kvg.shape), const(wk.shape), const(wvt.shape),
            pl.BlockSpec((s2, r, LANES), lambda bi, ri: (0, ri, 0)),
        ],
        out_specs=[
            pl.BlockSpec((1, 2, s2, r, F_WIDTH), lambda bi, ri: (bi, 0, 0, ri, 0)),
            pl.BlockSpec((1, N_HEADS, s2, r, HEAD_PAD), lambda bi, ri: (bi, 0, 0, ri, 0)),
            pl.BlockSpec((1, N_HEADS, tm, HEAD_PAD), lambda bi, ri: (bi, 0, ri, 0)),
            pl.BlockSpec((1, N_HEADS, 1, V_ROWS, tm), lambda bi, ri: (bi, 0, ri // cpk, 0, ri % cpk)),
        ],
        out_shape=[jax.ShapeDtypeStruct((b, 2, s2, s1, F_WIDTH), BF16),
                   jax.ShapeDtypeStruct((b, N_HEADS, s2, s1, HEAD_PAD), BF16),
                   jax.ShapeDtypeStruct((b, N_HEADS, s, HEAD_PAD), BF16),
                   jax.ShapeDtypeStruct((b, N_HEADS, s // tk, V_ROWS, tk), BF16)],
        compiler_params=pltpu.CompilerParams(
            dimension_semantics=("arbitrary", "arbitrary"), vmem_limit_bytes=VMEM_LIMITS["proj"]),
        name="proj",
    )(x.reshape(b, s2, s1, d), g, w1, d64, kc, ks, qg, wq, kvg, wk, wvt, cs.reshape(s2, s1, LANES))
    return v1, q.reshape(b, N_HEADS, s, HEAD_PAD), k, vt


def _fft2_kernel(mc_ref, ms_ref, vr_ref, vi_ref, y_ref):
    for j in range(mc_ref.shape[0]):
        y = _dot(mc_ref[j], vr_ref[0, 0, j]) + _dot(ms_ref[j], vi_ref[0, 0, j])
        y_ref[0, :, j * F_WIDTH:(j + 1) * F_WIDTH] = y.astype(BF16)


def _fft2(mc, ms, v, *, kb):
    b, _, s2, s1, f = v.shape
    m_spec = pl.BlockSpec((kb, s1, s1), lambda ki, bi: (ki, 0, 0))
    return pl.pallas_call(
        _fft2_kernel,
        grid=(s2 // kb, b),
        in_specs=[m_spec, m_spec,
                  pl.BlockSpec((1, 1, kb, s1, f), lambda ki, bi: (bi, 0, ki, 0, 0)),
                  pl.BlockSpec((1, 1, kb, s1, f), lambda ki, bi: (bi, 1, ki, 0, 0))],
        out_specs=pl.BlockSpec((1, s1, kb * f), lambda ki, bi: (bi, 0, ki)),
        out_shape=jax.ShapeDtypeStruct((b, s1, s2 * f), BF16),
        compiler_params=pltpu.CompilerParams(
            dimension_semantics=("arbitrary", "arbitrary"), vmem_limit_bytes=VMEM_LIMITS["fft2"]),
        name="fft2",
    )(mc, ms, v, v)


ATTN_UNROLL = 16
ATTN_AHEAD = 2
ATTN_SLOTS = 4


def _attn_kernel(q_ref, k_ref, vt_ref, o_ref, s_s, mx_s, acc_s, m_s, *, tq, tk):
    hb = q_ref.shape[1]
    nq = q_ref.shape[2] // tq
    nk = k_ref.shape[2] // tk
    nsteps = hb * nq * nk
    assert nsteps % ATTN_UNROLL == 0 and ATTN_UNROLL % nk == 0 and ATTN_UNROLL % ATTN_SLOTS == 0

    def tile(t0, i):
        r = lax.div(t0, nk) + i // nk
        return lax.div(r, nq), pl.multiple_of(lax.rem(r, nq) * tq, tq), i % nk

    def scores(t0, i):
        hh, qs, kj = tile(t0, i)
        s = lax.dot_general(k_ref[0, hh, kj * tk:(kj + 1) * tk, :], q_ref[0, hh, pl.ds(qs, tq), :],
                            (((1,), (1,)), ((), ())), preferred_element_type=F32)
        s_s[i % ATTN_SLOTS] = s
        mx_s[i % ATTN_SLOTS] = jnp.max(s.reshape(tk // SUBLANES, SUBLANES, tq), axis=0)

    def softmax_pv(t0, i):
        hh, qs, kj = tile(t0, i)
        slot = i % ATTN_SLOTS
        mx = jnp.max(mx_s[slot], axis=0, keepdims=True)
        if kj == 0:
            m_new = mx
        else:
            m_prev = m_s[...]
            m_new = jnp.maximum(m_prev, mx)
        p = jnp.exp2(s_s[slot] - m_new).astype(BF16)
        acc = _dot(vt_ref[0, hh, kj], p)
        if kj > 0:
            acc = jnp.exp2(m_prev - m_new) * acc_s[...] + acc
        if kj == nk - 1:
            o = acc[:V_HEAD_DIM] / acc[V_HEAD_DIM:V_HEAD_DIM + 1]
            o_ref[0, hh, pl.ds(qs, tq), :] = o.T.astype(BF16)
        else:
            acc_s[...] = acc
            m_s[...] = m_new

    def steps(t0, last):
        for i in range(ATTN_UNROLL):
            if not (last and i + ATTN_AHEAD >= ATTN_UNROLL):
                scores(t0, i + ATTN_AHEAD)
            softmax_pv(t0, i)

    for i in range(ATTN_AHEAD):
        scores(jnp.int32(0), i)

    def body(u, carry):
        steps(u * ATTN_UNROLL, False)
        return carry

    lax.fori_loop(0, nsteps // ATTN_UNROLL - 1, body, 0)
    steps(jnp.int32(nsteps - ATTN_UNROLL), True)


def _attention(q, k, vt, *, hb, tq):
    b, h, s, w = q.shape
    nk, v_rows, tk = vt.shape[2:]
    spec = pl.BlockSpec((1, hb, s, w), lambda bi, hi: (bi, hi, 0, 0))
    return pl.pallas_call(
        functools.partial(_attn_kernel, tq=tq, tk=tk),
        grid=(b, h // hb),
        in_specs=[spec, spec, pl.BlockSpec((1, hb, nk, v_rows, tk), lambda bi, hi: (bi, hi, 0, 0, 0))],
        out_specs=pl.BlockSpec((1, hb, s, V_HEAD_DIM), lambda bi, hi: (bi, hi, 0, 0)),
        out_shape=jax.ShapeDtypeStruct((b, h, s, V_HEAD_DIM), BF16),
        scratch_shapes=[pltpu.VMEM((ATTN_SLOTS, tk, tq), F32), pltpu.VMEM((ATTN_SLOTS, SUBLANES, tq), F32),
                        pltpu.VMEM((v_rows, tq), F32), pltpu.VMEM((1, tq), F32)],
        compiler_params=pltpu.CompilerParams(
            dimension_semantics=("arbitrary", "arbitrary"), vmem_limit_bytes=VMEM_LIMITS["attention"]),
        name="attention",
    )(q, k, vt)


def _ffn_kernel(x_ref, yf_ref, ya_ref, wof_ref, woa_ref, g2_ref, wg_ref, wu_ref, wd_ref, gf_ref, o_ref, *, tf):
    ya = jnp.concatenate([ya_ref[0, hh] for hh in range(ya_ref.shape[1])], axis=1)
    x1 = x_ref[0] + _dot(yf_ref[0], wof_ref[...]) + _dot(ya, woa_ref[...])
    xn = _rms(x1, g2_ref[...]).astype(BF16)
    ffn = None
    for c in range(0, wg_ref.shape[1], tf):
        gate = _dot(xn, wg_ref[:, c:c + tf])
        up = _dot(xn, wu_ref[:, c:c + tf])
        hmid = (gate * jax.nn.sigmoid(gate) * up).astype(BF16)
        part = _dot(hmid, wd_ref[c:c + tf, :])
        ffn = part if ffn is None else ffn + part
    o_ref[0] = _rms(x1 + ffn, gf_ref[...])


def _ffn(x, yf, ya, wof, woa, g2, wg, wu, wd, gf, *, tm, tf):
    b, s, d = x.shape
    h = ya.shape[1]
    assert wg.shape[1] % tf == 0
    const = lambda shape: pl.BlockSpec(shape, lambda bi, si: (0,) * len(shape), pipeline_mode=pl.Buffered(1))
    row = lambda width: pl.BlockSpec((1, tm, width), lambda bi, si: (bi, si, 0))
    return pl.pallas_call(
        functools.partial(_ffn_kernel, tf=tf),
        grid=(b, s // tm),
        in_specs=[row(d), row(yf.shape[2]),
                  pl.BlockSpec((1, h, tm, V_HEAD_DIM), lambda bi, si: (bi, 0, si, 0)),
                  const(wof.shape), const(woa.shape), const(g2.shape), const(wg.shape), const(wu.shape),
                  const(wd.shape), const(gf.shape)],
        out_specs=row(d),
        out_shape=jax.ShapeDtypeStruct((b, s, d), F32),
        compiler_params=pltpu.CompilerParams(
            dimension_semantics=("arbitrary", "arbitrary"), vmem_limit_bytes=VMEM_LIMITS["ffn"]),
        name="ffn",
    )(x, yf, ya, wof, woa, g2, wg, wu, wd, gf)


def _swap_halves(w):
    half = w.shape[-1] // 2
    return jnp.concatenate([-w[..., half:], w[..., :half]], axis=-1)


def _rope_table(seq_len):
    inv_freq = 1.0 / (ROPE_THETA ** (jnp.arange(0, QK_ROPE_DIM, 2, dtype=F32) / QK_ROPE_DIM))
    ang = jnp.arange(seq_len, dtype=F32)[:, None] * inv_freq[None, :]
    cos, sin = jnp.cos(ang), jnp.sin(ang)
    return jnp.concatenate([cos, cos, sin, sin], axis=-1)


def _cos_sin(num, den):
    ang = (2.0 * math.pi / den) * (num % den).astype(F32)
    return jnp.cos(ang), jnp.sin(ang)


def _channel_dft():
    j = jnp.arange(FGROUP_DIM, dtype=jnp.int32)
    c, s = _cos_sin(j[:, None] * j[None, :], FGROUP_DIM)
    eye = jnp.eye(N_FGROUPS, dtype=F32)
    scale = 1.0 / math.sqrt(FGROUP_DIM)
    re, im = jnp.kron(eye, c * scale), jnp.kron(eye, -s * scale)
    return jnp.concatenate([re, im], axis=1).astype(BF16)


def _seq_dft_tables(seq_len):
    s1 = FFT_S1
    s2 = seq_len // s1
    i2 = jnp.arange(s2, dtype=jnp.int32)
    c2, sn2 = _cos_sin(i2[:, None] * i2[None, :], s2)
    sc2 = 1.0 / math.sqrt(s2)
    eye_r = jnp.eye(MXU_DIM // s2, dtype=F32)
    kc, ks = jnp.kron(c2 * sc2, eye_r), jnp.kron(sn2 * sc2, eye_r)
    i1 = jnp.arange(s1, dtype=jnp.int32)
    ca, sa = _cos_sin(i1[:, None] * i1[None, :], s1)
    cb, sb = _cos_sin(i2[:, None] * i1[None, :], seq_len)
    sc1 = 1.0 / math.sqrt(s1)
    mc = (ca[None] * cb[:, None, :] - sa[None] * sb[:, None, :]) * sc1
    ms = (sa[None] * cb[:, None, :] + ca[None] * sb[:, None, :]) * sc1
    return kc.astype(BF16), ks.astype(BF16), mc.astype(BF16), ms.astype(BF16)


def _prep_weights(norm_mix_g, w_in, q_norm_g, w_q_up, kv_norm_g, w_kv_up, w_out, norm_ffn_g,
                  w_gate, w_up, w_down, final_norm_g):
    q_rank = q_norm_g.shape[-1]
    kv_rank = kv_norm_g.shape[-1]
    o3 = F_WIDTH + q_rank + kv_rank
    w1 = jnp.concatenate([w_in, _swap_halves(w_in[:, o3:])], axis=1).astype(BF16)
    wq = w_q_up.reshape(q_rank, N_HEADS, QK_HEAD_DIM)
    wq_rope = wq[..., QK_NOPE_DIM:]
    wq = jnp.concatenate([wq, _swap_halves(wq_rope)], axis=-1).reshape(q_rank, N_HEADS * HEAD_PAD)
    wkv = w_kv_up.reshape(kv_rank, N_HEADS, QK_NOPE_DIM + V_HEAD_DIM)
    wk = wkv[..., :QK_NOPE_DIM].reshape(kv_rank, -1)
    wvt = wkv[..., QK_NOPE_DIM:].reshape(kv_rank, -1).T
    return dict(
        g1=norm_mix_g[None, :], w1=w1, d64=_channel_dft(), qg=q_norm_g[None, :], wq=wq.astype(BF16),
        kvg=kv_norm_g[None, :], wk=wk.astype(BF16), wvt=wvt.astype(BF16),
        wof=w_out[:F_WIDTH].astype(BF16), woa=w_out[F_WIDTH:].astype(BF16), g2=norm_ffn_g[None, :],
        wg=w_gate.astype(BF16), wu=w_up.astype(BF16), wd=w_down.astype(BF16), gf=final_norm_g[None, :])


def _trunk(x, w):
    b, s, d = x.shape
    t = _tiles(s)
    kc, ks, mc, ms = _seq_dft_tables(s)
    v1, q, k, vt = _proj(x, w["g1"], w["w1"], w["d64"], kc, ks, w["qg"], w["wq"], w["kvg"], w["wk"], w["wvt"],
                         _rope_table(s), tm=t["proj_tokens"], tk=t["attn_kv"])
    yf = _fft2(mc, ms, v1, kb=t["fft2_k2"])
    ya = _attention(q, k, vt, hb=t["attn_heads"], tq=t["attn_q"])
    return _ffn(x, yf.reshape(b, s, F_WIDTH), ya, w["wof"], w["woa"], w["g2"], w["wg"], w["wu"], w["wd"],
                w["gf"], tm=t["ffn_tokens"], tf=MXU_DIM)


def kernel(x_prompt, x_sample, norm_mix_g, w_in, q_norm_g, w_q_up, kv_norm_g, w_kv_up, w_out, norm_ffn_g,
           w_gate, w_up, w_down, final_norm_g):
    assert w_in.shape[0] == 1, "single-layer trunk only"
    w = _prep_weights(norm_mix_g[0], w_in[0], q_norm_g[0], w_q_up[0], kv_norm_g[0], w_kv_up[0], w_out[0],
                      norm_ffn_g[0], w_gate[0], w_up[0], w_down[0], final_norm_g)
    return (_trunk(x_prompt, w), _trunk(x_sample, w))
```

```python
import functools
import math

import jax
import jax.numpy as jnp
from jax import lax
from jax.experimental import pallas as pl
from jax.experimental.pallas import tpu as pltpu

LANES = 128
SUBLANES = 8
MXU_DIM = 256
VMEM_BYTES = 64 * 1024 * 1024

N_FGROUPS = 4
FGROUP_DIM = 64
F_WIDTH = N_FGROUPS * FGROUP_DIM
N_HEADS = 6
QK_NOPE_DIM = 128
QK_ROPE_DIM = 64
V_HEAD_DIM = 128
QK_HEAD_DIM = QK_NOPE_DIM + QK_ROPE_DIM
ROPE_THETA = 10000.0
EPS = 1e-6
SM_SCALE = 1.0 / math.sqrt(QK_HEAD_DIM)
LOG2E = 1.4426950408889634

HEAD_PAD = 2 * LANES
BF16_SUBLANES = 16
V_ROWS = V_HEAD_DIM + BF16_SUBLANES
FFT_S1 = MXU_DIM

BF16 = jnp.bfloat16
F32 = jnp.float32


MIB = 1024 * 1024
VMEM_LIMITS = dict(proj=48 * MIB, fft2=32 * MIB, attention=56 * MIB, ffn=56 * MIB)
assert max(VMEM_LIMITS.values()) < VMEM_BYTES


def _tiles(seq_len):
    return dict(
        proj_tokens=1024,
        ffn_tokens=1024,
        fft2_k2=8,
        attn_q=MXU_DIM,
        attn_kv=min(seq_len, 2048),
        attn_heads=1 if seq_len > 2048 else N_HEADS,
    )


def _rms(x, g):
    return x * lax.rsqrt(jnp.mean(x * x, axis=-1, keepdims=True) + EPS) * g


def _dot(a, b):
    return jnp.dot(a, b, preferred_element_type=F32)


def _proj_kernel(x_ref, g_ref, w1_ref, d64_ref, kc_ref, ks_ref, qg_ref, wq_ref, kvg_ref, wk_ref, wvt_ref, cs_ref,
                 v1_ref, q_ref, k_ref, vt_ref):
    s2, r, d = x_ref.shape[1:]
    tm = s2 * r
    o1 = F_WIDTH
    o2 = o1 + qg_ref.shape[1]
    o3 = o2 + kvg_ref.shape[1]
    xn = _rms(x_ref[0].reshape(tm, d), g_ref[...]).astype(BF16)
    h = _dot(xn, w1_ref[...])

    z = _dot(h[:, :o1].astype(BF16), d64_ref[...])
    za = z.reshape(s2, r, 2 * o1)
    zb = jnp.concatenate([z[:, o1:], -z[:, :o1]], axis=1).reshape(s2, r, 2 * o1)
    rs = kc_ref.shape[0] // s2
    parts = []
    for j in range(0, r, rs):
        sub = lambda a: a[:, j:j + rs, :].reshape(s2 * rs, 2 * o1).astype(BF16)
        parts.append((_dot(kc_ref[...], sub(za)) + _dot(ks_ref[...], sub(zb))).reshape(s2, rs, 2 * o1))
    v1 = jnp.concatenate(parts, axis=1)
    v1_ref[0, 0] = v1[:, :, :o1].astype(BF16)
    v1_ref[0, 1] = v1[:, :, o1:].astype(BF16)

    cs = cs_ref[...].reshape(tm, LANES)
    cqn = (_rms(h[:, o1:o2], qg_ref[...]) * (SM_SCALE * LOG2E)).astype(BF16)
    qf = _dot(cqn, wq_ref[...])
    for hh in range(N_HEADS):
        base = hh * HEAD_PAD
        q_ref[0, hh, :, :, :LANES] = qf[:, base:base + LANES].astype(BF16).reshape(s2, r, LANES)
        q_ref[0, hh, :, :, LANES:] = (qf[:, base + LANES:base + HEAD_PAD] * cs).astype(BF16).reshape(s2, r, LANES)

    ckv = _rms(h[:, o2:o3], kvg_ref[...])
    kn = _dot(ckv.astype(BF16), wk_ref[...])
    vt = _dot(wvt_ref[...], ckv.T.astype(BF16))
    pk = h[:, o3:] * cs
    kr = (pk + pltpu.roll(pk, QK_ROPE_DIM, axis=1)).astype(BF16)
    ones = jnp.ones((V_ROWS - V_HEAD_DIM, tm), BF16)
    for hh in range(N_HEADS):
        k_ref[0, hh, :, :LANES] = kn[:, hh * LANES:(hh + 1) * LANES].astype(BF16)
        k_ref[0, hh, :, LANES:] = kr
        vt_ref[0, hh, 0, :V_HEAD_DIM, :] = vt[hh * V_HEAD_DIM:(hh + 1) * V_HEAD_DIM, :].astype(BF16)
        vt_ref[0, hh, 0, V_HEAD_DIM:, :] = ones


def _proj(x, g, w1, d64, kc, ks, qg, wq, kvg, wk, wvt, cs, *, tm, tk):
    b, s, d = x.shape
    s1 = FFT_S1
    s2 = s // s1
    r = tm // s2
    cpk = tk // tm
    const = lambda shape: pl.BlockSpec(shape, lambda bi, ri: (0,) * len(shape))
    v1, q, k, vt = pl.pallas_call(
        _proj_kernel,
        grid=(b, s1 // r),
        in_specs=[
            pl.BlockSpec((1, s2, r, d), lambda bi, ri: (bi, 0, ri, 0)),
            const(g.shape), const(w1.shape), const(d64.shape), const(kc.shape), const(ks.shape),
            const(qg.shape), const(wq.shape), const(kvg.shape), const(wk.shape), const(wvt.shape),
            pl.BlockSpec((s2, r, LANES), lambda bi, ri: (0, ri, 0)),
        ],
        out_specs=[
            pl.BlockSpec((1, 2, s2, r, F_WIDTH), lambda bi, ri: (bi, 0, 0, ri, 0)),
            pl.BlockSpec((1, N_HEADS, s2, r, HEAD_PAD), lambda bi, ri: (bi, 0, 0, ri, 0)),
            pl.BlockSpec((1, N_HEADS, tm, HEAD_PAD), lambda bi, ri: (bi, 0, ri, 0)),
            pl.BlockSpec((1, N_HEADS, 1, V_ROWS, tm), lambda bi, ri: (bi, 0, ri // cpk, 0, ri % cpk)),
        ],
        out_shape=[jax.ShapeDtypeStruct((b, 2, s2, s1, F_WIDTH), BF16),
                   jax.ShapeDtypeStruct((b, N_HEADS, s2, s1, HEAD_PAD), BF16),
                   jax.ShapeDtypeStruct((b, N_HEADS, s, HEAD_PAD), BF16),
                   jax.ShapeDtypeStruct((b, N_HEADS, s // tk, V_ROWS, tk), BF16)],
        compiler_params=pltpu.CompilerParams(
            dimension_semantics=("arbitrary", "arbitrary"), vmem_limit_bytes=VMEM_LIMITS["proj"]),
        name="proj",
    )(x.reshape(b, s2, s1, d), g, w1, d64, kc, ks, qg, wq, kvg, wk, wvt, cs.reshape(s2, s1, LANES))
    return v1, q.reshape(b, N_HEADS, s, HEAD_PAD), k, vt


def _fft2_kernel(mc_ref, ms_ref, vr_ref, vi_ref, y_ref):
    for j in range(mc_ref.shape[0]):
        y = _dot(mc_ref[j], vr_ref[0, 0, j]) + _dot(ms_ref[j], vi_ref[0, 0, j])
        y_ref[0, :, j * F_WIDTH:(j + 1) * F_WIDTH] = y.astype(BF16)


def _fft2(mc, ms, v, *, kb):
    b, _, s2, s1, f = v.shape
    m_spec = pl.BlockSpec((kb, s1, s1), lambda ki, bi: (ki, 0, 0))
    return pl.pallas_call(
        _fft2_kernel,
        grid=(s2 // kb, b),
        in_specs=[m_spec, m_spec,
                  pl.BlockSpec((1, 1, kb, s1, f), lambda ki, bi: (bi, 0, ki, 0, 0)),
                  pl.BlockSpec((1, 1, kb, s1, f), lambda ki, bi: (bi, 1, ki, 0, 0))],
        out_specs=pl.BlockSpec((1, s1, kb * f), lambda ki, bi: (bi, 0, ki)),
        out_shape=jax.ShapeDtypeStruct((b, s1, s2 * f), BF16),
        compiler_params=pltpu.CompilerParams(
            dimension_semantics=("arbitrary", "arbitrary"), vmem_limit_bytes=VMEM_LIMITS["fft2"]),
        name="fft2",
    )(mc, ms, v, v)


ATTN_UNROLL = 16
ATTN_AHEAD = 2
ATTN_SLOTS = 4


def _attn_kernel(q_ref, k_ref, vt_ref, o_ref, s_s, mx_s, acc_s, m_s, *, tq, tk):
    hb = q_ref.shape[1]
    nq = q_ref.shape[2] // tq
    nk = k_ref.shape[2] // tk
    nsteps = hb * nq * nk
    assert nsteps % ATTN_UNROLL == 0 and ATTN_UNROLL % nk == 0 and ATTN_UNROLL % ATTN_SLOTS == 0

    def tile(t0, i):
        r = lax.div(t0, nk) + i // nk
        return lax.div(r, nq), pl.multiple_of(lax.rem(r, nq) * tq, tq), i % nk

    def scores(t0, i):
        hh, qs, kj = tile(t0, i)
        s = lax.dot_general(k_ref[0, hh, kj * tk:(kj + 1) * tk, :], q_ref[0, hh, pl.ds(qs, tq), :],
                            (((1,), (1,)), ((), ())), preferred_element_type=F32)
        s_s[i % ATTN_SLOTS] = s
        mx_s[i % ATTN_SLOTS] = jnp.max(s.reshape(tk // SUBLANES, SUBLANES, tq), axis=0)

    def softmax_pv(t0, i):
        hh, qs, kj = tile(t0, i)
        slot = i % ATTN_SLOTS
        mx = jnp.max(mx_s[slot], axis=0, keepdims=True)
        if kj == 0:
            m_new = mx
        else:
            m_prev = m_s[...]
            m_new = jnp.maximum(m_prev, mx)
        p = jnp.exp2(s_s[slot] - m_new).astype(BF16)
        acc = _dot(vt_ref[0, hh, kj], p)
        if kj > 0:
            acc = jnp.exp2(m_prev - m_new) * acc_s[...] + acc
        if kj == nk - 1:
            o = acc[:V_HEAD_DIM] / acc[V_HEAD_DIM:V_HEAD_DIM + 1]
            o_ref[0, hh, pl.ds(qs, tq), :] = o.T.astype(BF16)
        else:
            acc_s[...] = acc
            m_s[...] = m_new

    def steps(t0, last):
        for i in range(ATTN_UNROLL):
            if not (last and i + ATTN_AHEAD >= ATTN_UNROLL):
                scores(t0, i + ATTN_AHEAD)
            softmax_pv(t0, i)

    for i in range(ATTN_AHEAD):
        scores(jnp.int32(0), i)

    def body(u, carry):
        steps(u * ATTN_UNROLL, False)
        return carry

    lax.fori_loop(0, nsteps // ATTN_UNROLL - 1, body, 0)
    steps(jnp.int32(nsteps - ATTN_UNROLL), True)


def _attention(q, k, vt, *, hb, tq):
    b, h, s, w = q.shape
    nk, v_rows, tk = vt.shape[2:]
    spec = pl.BlockSpec((1, hb, s, w), lambda bi, hi: (bi, hi, 0, 0))
    return pl.pallas_call(
        functools.partial(_attn_kernel, tq=tq, tk=tk),
        grid=(b, h // hb),
        in_specs=[spec, spec, pl.BlockSpec((1, hb, nk, v_rows, tk), lambda bi, hi: (bi, hi, 0, 0, 0))],
        out_specs=pl.BlockSpec((1, hb, s, V_HEAD_DIM), lambda bi, hi: (bi, hi, 0, 0)),
        out_shape=jax.ShapeDtypeStruct((b, h, s, V_HEAD_DIM), BF16),
        scratch_shapes=[pltpu.VMEM((ATTN_SLOTS, tk, tq), F32), pltpu.VMEM((ATTN_SLOTS, SUBLANES, tq), F32),
                        pltpu.VMEM((v_rows, tq), F32), pltpu.VMEM((1, tq), F32)],
        compiler_params=pltpu.CompilerParams(
            dimension_semantics=("arbitrary", "arbitrary"), vmem_limit_bytes=VMEM_LIMITS["attention"]),
        name="attention",
    )(q, k, vt)


def _ffn_kernel(x_ref, yf_ref, ya_ref, wof_ref, woa_ref, g2_ref, wg_ref, wu_ref, wd_ref, gf_ref, o_ref, *, tf):
    ya = jnp.concatenate([ya_ref[0, hh] for hh in range(ya_ref.shape[1])], axis=1)
    rows, width = yf_ref.shape[1:]
    yf = yf_ref[0].reshape(rows, width // F_WIDTH, F_WIDTH).reshape(x_ref.shape[1], F_WIDTH)
    x1 = x_ref[0] + _dot(yf, wof_ref[...]) + _dot(ya, woa_ref[...])
    xn = _rms(x1, g2_ref[...]).astype(BF16)
    ffn = None
    for c in range(0, wg_ref.shape[1], tf):
        gate = _dot(xn, wg_ref[:, c:c + tf])
        up = _dot(xn, wu_ref[:, c:c + tf])
        hmid = (gate * jax.nn.sigmoid(gate) * up).astype(BF16)
        part = _dot(hmid, wd_ref[c:c + tf, :])
        ffn = part if ffn is None else ffn + part
    o_ref[0] = _rms(x1 + ffn, gf_ref[...])


def _ffn(x, yf, ya, wof, woa, g2, wg, wu, wd, gf, *, tm, tf):
    b, s, d = x.shape
    h = ya.shape[1]
    assert wg.shape[1] % tf == 0
    const = lambda shape: pl.BlockSpec(shape, lambda bi, si: (0,) * len(shape), pipeline_mode=pl.Buffered(1))
    row = lambda width: pl.BlockSpec((1, tm, width), lambda bi, si: (bi, si, 0))
    return pl.pallas_call(
        functools.partial(_ffn_kernel, tf=tf),
        grid=(b, s // tm),
        in_specs=[row(d), pl.BlockSpec((1, tm * F_WIDTH // yf.shape[2], yf.shape[2]), lambda bi, si: (bi, si, 0)),
                  pl.BlockSpec((1, h, tm, V_HEAD_DIM), lambda bi, si: (bi, 0, si, 0)),
                  const(wof.shape), const(woa.shape), const(g2.shape), const(wg.shape), const(wu.shape),
                  const(wd.shape), const(gf.shape)],
        out_specs=row(d),
        out_shape=jax.ShapeDtypeStruct((b, s, d), F32),
        compiler_params=pltpu.CompilerParams(
            dimension_semantics=("arbitrary", "arbitrary"), vmem_limit_bytes=VMEM_LIMITS["ffn"]),
        name="ffn",
    )(x, yf, ya, wof, woa, g2, wg, wu, wd, gf)


def _swap_halves(w):
    half = w.shape[-1] // 2
    return jnp.concatenate([-w[..., half:], w[..., :half]], axis=-1)


def _rope_table(seq_len):
    inv_freq = 1.0 / (ROPE_THETA ** (jnp.arange(0, QK_ROPE_DIM, 2, dtype=F32) / QK_ROPE_DIM))
    ang = jnp.arange(seq_len, dtype=F32)[:, None] * inv_freq[None, :]
    cos, sin = jnp.cos(ang), jnp.sin(ang)
    return jnp.concatenate([cos, cos, sin, sin], axis=-1)


def _cos_sin(num, den):
    ang = (2.0 * math.pi / den) * (num % den).astype(F32)
    return jnp.cos(ang), jnp.sin(ang)


def _channel_dft():
    j = jnp.arange(FGROUP_DIM, dtype=jnp.int32)
    c, s = _cos_sin(j[:, None] * j[None, :], FGROUP_DIM)
    eye = jnp.eye(N_FGROUPS, dtype=F32)
    scale = 1.0 / math.sqrt(FGROUP_DIM)
    re, im = jnp.kron(eye, c * scale), jnp.kron(eye, -s * scale)
    return jnp.concatenate([re, im], axis=1).astype(BF16)


def _seq_dft_tables(seq_len):
    s1 = FFT_S1
    s2 = seq_len // s1
    i2 = jnp.arange(s2, dtype=jnp.int32)
    c2, sn2 = _cos_sin(i2[:, None] * i2[None, :], s2)
    sc2 = 1.0 / math.sqrt(s2)
    eye_r = jnp.eye(MXU_DIM // s2, dtype=F32)
    kc, ks = jnp.kron(c2 * sc2, eye_r), jnp.kron(sn2 * sc2, eye_r)
    i1 = jnp.arange(s1, dtype=jnp.int32)
    ca, sa = _cos_sin(i1[:, None] * i1[None, :], s1)
    cb, sb = _cos_sin(i2[:, None] * i1[None, :], seq_len)
    sc1 = 1.0 / math.sqrt(s1)
    mc = (ca[None] * cb[:, None, :] - sa[None] * sb[:, None, :]) * sc1
    ms = (sa[None] * cb[:, None, :] + ca[None] * sb[:, None, :]) * sc1
    return kc.astype(BF16), ks.astype(BF16), mc.astype(BF16), ms.astype(BF16)


def _prep_weights(norm_mix_g, w_in, q_norm_g, w_q_up, kv_norm_g, w_kv_up, w_out, norm_ffn_g,
                  w_gate, w_up, w_down, final_norm_g):
    q_rank = q_norm_g.shape[-1]
    kv_rank = kv_norm_g.shape[-1]
    o3 = F_WIDTH + q_rank + kv_rank
    w1 = jnp.concatenate([w_in, _swap_halves(w_in[:, o3:])], axis=1).astype(BF16)
    wq = w_q_up.reshape(q_rank, N_HEADS, QK_HEAD_DIM)
    wq_rope = wq[..., QK_NOPE_DIM:]
    wq = jnp.concatenate([wq, _swap_halves(wq_rope)], axis=-1).reshape(q_rank, N_HEADS * HEAD_PAD)
    wkv = w_kv_up.reshape(kv_rank, N_HEADS, QK_NOPE_DIM + V_HEAD_DIM)
    wk = wkv[..., :QK_NOPE_DIM].reshape(kv_rank, -1)
    wvt = wkv[..., QK_NOPE_DIM:].reshape(kv_rank, -1).T
    return dict(
        g1=norm_mix_g[None, :], w1=w1, d64=_channel_dft(), qg=q_norm_g[None, :], wq=wq.astype(BF16),
        kvg=kv_norm_g[None, :], wk=wk.astype(BF16), wvt=wvt.astype(BF16),
        wof=w_out[:F_WIDTH].astype(BF16), woa=w_out[F_WIDTH:].astype(BF16), g2=norm_ffn_g[None, :],
        wg=w_gate.astype(BF16), wu=w_up.astype(BF16), wd=w_down.astype(BF16), gf=final_norm_g[None, :])


def _trunk(x, w):
    b, s, d = x.shape
    t = _tiles(s)
    kc, ks, mc, ms = _seq_dft_tables(s)
    v1, q, k, vt = _proj(x, w["g1"], w["w1"], w["d64"], kc, ks, w["qg"], w["wq"], w["kvg"], w["wk"], w["wvt"],
                         _rope_table(s), tm=t["proj_tokens"], tk=t["attn_kv"])
    yf = _fft2(mc, ms, v1, kb=t["fft2_k2"])
    ya = _attention(q, k, vt, hb=t["attn_heads"], tq=t["attn_q"])
    return _ffn(x, yf, ya, w["wof"], w["woa"], w["g2"], w["wg"], w["wu"], w["wd"],
                w["gf"], tm=t["ffn_tokens"], tf=MXU_DIM)


def kernel(x_prompt, x_sample, norm_mix_g, w_in, q_norm_g, w_q_up, kv_norm_g, w_kv_up, w_out, norm_ffn_g,
           w_gate, w_up, w_down, final_norm_g):
    assert w_in.shape[0] == 1, "single-layer trunk only"
    w = _prep_weights(norm_mix_g[0], w_in[0], q_norm_g[0], w_q_up[0], kv_norm_g[0], w_kv_up[0], w_out[0],
                      norm_ffn_g[0], w_gate[0], w_up[0], w_down[0], final_norm_g)
    return (_trunk(x_prompt, w), _trunk(x_sample, w))
```
